```python
import jax, jax.numpy as jnp
from jax import lax
import numpy as np

D_MODEL = 1024
BATCH = 2
SEQ = 8192
DEPTH = 2
DEC_BATCH = 128
DEC_SEQ = 8
PAST_LEN = 2048
PAGE_SIZE = 128

N_MIXERS = 2
N_GLA_LAYERS = (DEPTH + 1) // 2
N_SB_LAYERS = DEPTH // 2
GLA_HEADS = 4
GLA_DK = D_MODEL // 2 // GLA_HEADS
GLA_DV = D_MODEL // GLA_HEADS
GLA_KEY_W = GLA_HEADS * GLA_DK
GLA_VAL_W = GLA_HEADS * GLA_DV
GLA_GATE_RANK = 16
GLA_TAU = 16.0
GLA_CHUNK = 64
GLA_IN_W = 2 * GLA_KEY_W + 2 * GLA_VAL_W + GLA_GATE_RANK
SB_HEADS = 16
SB_DH = D_MODEL // SB_HEADS
SB_W = SB_HEADS * SB_DH
SB_IN_W = 4 * SB_W
SB_QBLOCK = 128
SB_BIAS_INIT = -6.0
EPS = 1e-6

kernel_name = 'hybrid_gla_stickbreaking_decoder_step'


def rms_norm(x, gain):
    xf = x.astype(jnp.float32)
    y = xf * lax.rsqrt(jnp.mean(xf * xf, axis=-1, keepdims=True) + EPS)
    return (y * gain.astype(jnp.float32)).astype(x.dtype)


def gla_recurrence(q, k, v, log_a, s0):
    bsz, length, nh = q.shape[0], q.shape[1], q.shape[2]
    c = min(GLA_CHUNK, length)
    n = length // c

    def to_chunks(t):
        t = t.astype(jnp.float32).reshape(bsz, n, c, nh, t.shape[3])
        return t.transpose(1, 0, 3, 2, 4)

    causal = jnp.tril(jnp.ones((c, c), dtype=bool))

    def step(s, inp):
        qc, kc, vc, gc = inp
        b = jnp.cumsum(gc, axis=-2)
        b_last = b[..., -1:, :]
        qd = qc * jnp.exp(b)
        kd = kc * jnp.exp(-b)
        att = jnp.where(causal, jnp.einsum('bhtk,bhsk->bhts', qd, kd), 0.0)
        o = jnp.einsum('bhtk,bhkv->bhtv', qd, s) + jnp.einsum('bhts,bhsv->bhtv', att, vc)
        s_new = s * jnp.exp(b_last)[..., 0, :, None] + jnp.einsum(
            'bhsk,bhsv->bhkv', kc * jnp.exp(b_last - b), vc)
        return s_new, o

    s_fin, o = lax.scan(step, s0.astype(jnp.float32),
                        (to_chunks(q), to_chunks(k), to_chunks(v), to_chunks(log_a)))
    o = o.transpose(1, 0, 3, 2, 4).reshape(bsz, length, nh, v.shape[3])
    return o, s_fin


def gla_mixer(h, w_in, w_up, b_a, onorm, w_out, s0):
    bsz, length, _ = h.shape
    u = h @ w_in
    o1 = GLA_KEY_W
    o2 = 2 * GLA_KEY_W
    o3 = o2 + GLA_VAL_W
    o4 = o3 + GLA_VAL_W
    q = u[..., :o1].reshape(bsz, length, GLA_HEADS, GLA_DK) * (GLA_DK ** -0.5)
    k = u[..., o1:o2].reshape(bsz, length, GLA_HEADS, GLA_DK)
    v = u[..., o2:o3].reshape(bsz, length, GLA_HEADS, GLA_DV)
    g = u[..., o3:o4]
    a_code = u[..., o4:]
    log_a = jax.nn.log_sigmoid((a_code @ w_up + b_a).astype(jnp.float32)) / GLA_TAU
    log_a = log_a.reshape(bsz, length, GLA_HEADS, GLA_DK)
    o, s_fin = gla_recurrence(q, k, v, log_a, s0)
    o = rms_norm(o, onorm).reshape(bsz, length, GLA_VAL_W).astype(h.dtype)
    return (o * jax.nn.silu(g)) @ w_out, s_fin


def sb_attend(q, k, v, bias, q_offset):
    bsz, t_len, nh, dh = q.shape
    blk = min(SB_QBLOCK, t_len)
    nb = t_len // blk
    qb = q.reshape(bsz, nb, blk, nh, dh).transpose(1, 0, 2, 3, 4)
    q_pos = (q_offset + jnp.arange(t_len, dtype=jnp.int32)).reshape(nb, blk)
    k_pos = jnp.arange(k.shape[1], dtype=jnp.int32)
    bias_f = bias.astype(jnp.float32)[None, :, None, None]

    def one_block(args):
        qi, pi = args
        z = jnp.einsum('bqhd,bkhd->bhqk', qi, k).astype(jnp.float32) * (dh ** -0.5) + bias_f
        valid = k_pos[None, :] < pi[:, None]
        log_1mb = jnp.where(valid, jax.nn.log_sigmoid(-z), 0.0)
        rest = lax.cumsum(log_1mb, axis=3, reverse=True)
        a = jnp.exp(jnp.where(valid, z + rest, -jnp.inf))
        return jnp.einsum('bhqk,bkhd->bqhd', a.astype(v.dtype), v)

    o = lax.map(one_block, (qb, q_pos))
    return o.transpose(1, 0, 2, 3, 4).reshape(bsz, t_len, nh, dh)


def sb_mixer(h, w_in, qn, kn, bias, w_out, k_past, v_past):
    bsz, length, _ = h.shape
    u = h @ w_in
    q = rms_norm(u[..., :SB_W].reshape(bsz, length, SB_HEADS, SB_DH), qn)
    k = rms_norm(u[..., SB_W:2 * SB_W].reshape(bsz, length, SB_HEADS, SB_DH), kn)
    v = u[..., 2 * SB_W:3 * SB_W].reshape(bsz, length, SB_HEADS, SB_DH)
    g = u[..., 3 * SB_W:]
    if k_past is None:
        k_all, v_all, off = k, v, 0
    else:
        k_all = jnp.concatenate([k_past.astype(k.dtype), k], axis=1)
        v_all = jnp.concatenate([v_past.astype(v.dtype), v], axis=1)
        off = k_past.shape[1]
    o = sb_attend(q, k_all, v_all, bias, off).reshape(bsz, length, SB_W)
    return (o * jax.nn.silu(g)) @ w_out, k, v


def gather_pages(pool, page_table):
    rows = pool[page_table]
    return rows.reshape(rows.shape[0], rows.shape[1] * rows.shape[2], rows.shape[3], rows.shape[4])


def setup_inputs(seed: int = 0) -> dict:
    key = jax.random.key(seed)
    ks = jax.random.split(key, 17)
    n_pages = PAST_LEN // PAGE_SIZE
    n_used = DEC_BATCH * n_pages
    n_phys = n_used + (n_used + 3) // 4
    nrm = jax.random.normal
    f32 = jnp.float32
    x_prompt = nrm(ks[0], (BATCH, SEQ, D_MODEL), f32)
    x_sample = nrm(ks[1], (DEC_BATCH, DEC_SEQ, D_MODEL), f32)
    state_gla = 0.1 * nrm(ks[2], (N_GLA_LAYERS, DEC_BATCH, GLA_HEADS, GLA_DK, GLA_DV), f32)
    cache_k = nrm(ks[3], (N_SB_LAYERS, n_phys, PAGE_SIZE, SB_HEADS, SB_DH), f32)
    cache_v = nrm(ks[4], (N_SB_LAYERS, n_phys, PAGE_SIZE, SB_HEADS, SB_DH), f32)
    page_table = jax.random.permutation(ks[5], n_phys)[:n_used].reshape(DEC_BATCH, n_pages).astype(jnp.int32)
    norm_gain = 1.0 + 0.02 * nrm(ks[6], (DEPTH, D_MODEL), f32)
    w_in_a = nrm(ks[7], (N_GLA_LAYERS, D_MODEL, GLA_IN_W), f32) * D_MODEL ** -0.5
    w_alpha_up = nrm(ks[8], (N_GLA_LAYERS, GLA_GATE_RANK, GLA_KEY_W), f32) * GLA_GATE_RANK ** -0.5
    b_alpha = 0.1 * nrm(ks[9], (N_GLA_LAYERS, GLA_KEY_W), f32)
    onorm_a = 1.0 + 0.02 * nrm(ks[10], (N_GLA_LAYERS, GLA_DV), f32)
    w_out_a = nrm(ks[11], (N_GLA_LAYERS, GLA_VAL_W, D_MODEL), f32) * GLA_VAL_W ** -0.5
    w_in_b = nrm(ks[12], (N_SB_LAYERS, D_MODEL, SB_IN_W), f32) * D_MODEL ** -0.5
    qnorm_b = 1.0 + 0.02 * nrm(ks[13], (N_SB_LAYERS, SB_DH), f32)
    knorm_b = 1.0 + 0.02 * nrm(ks[14], (N_SB_LAYERS, SB_DH), f32)
    sb_bias = SB_BIAS_INIT + 0.1 * nrm(ks[16], (N_SB_LAYERS, SB_HEADS), f32)
    w_out_b = nrm(ks[15], (N_SB_LAYERS, SB_W, D_MODEL), f32) * SB_W ** -0.5
    return {'x_prompt': x_prompt, 'x_sample': x_sample, 'state_gla': state_gla,
            'cache_k': cache_k, 'cache_v': cache_v, 'page_table': page_table,
            'norm_gain': norm_gain, 'w_in_a': w_in_a, 'w_alpha_up': w_alpha_up,
            'b_alpha': b_alpha, 'onorm_a': onorm_a, 'w_out_a': w_out_a,
            'w_in_b': w_in_b, 'qnorm_b': qnorm_b, 'knorm_b': knorm_b, 'sb_bias': sb_bias,
            'w_out_b': w_out_b}


def reference(x_prompt, x_sample, state_gla, cache_k, cache_v, page_table,
              norm_gain, w_in_a, w_alpha_up, b_alpha, onorm_a, w_out_a,
              w_in_b, qnorm_b, knorm_b, sb_bias, w_out_b):
    xp, xs = x_prompt, x_sample
    sg_p, sg_s, k_p, v_p, k_s, v_s = [], [], [], [], [], []
    for i in range(DEPTH):
        j = i // N_MIXERS
        hp = rms_norm(xp, norm_gain[i])
        hs = rms_norm(xs, norm_gain[i])
        if i % N_MIXERS == 0:
            s0_p = jnp.zeros((xp.shape[0], GLA_HEADS, GLA_DK, GLA_DV), jnp.float32)
            yp, st_p = gla_mixer(hp, w_in_a[j], w_alpha_up[j], b_alpha[j], onorm_a[j], w_out_a[j], s0_p)
            ys, st_s = gla_mixer(hs, w_in_a[j], w_alpha_up[j], b_alpha[j], onorm_a[j], w_out_a[j], state_gla[j])
            sg_p.append(st_p)
            sg_s.append(st_s)
        else:
            yp, kp, vp = sb_mixer(hp, w_in_b[j], qnorm_b[j], knorm_b[j], sb_bias[j], w_out_b[j], None, None)
            k_past = gather_pages(cache_k[j], page_table)
            v_past = gather_pages(cache_v[j], page_table)
            ys, kn, vn = sb_mixer(hs, w_in_b[j], qnorm_b[j], knorm_b[j], sb_bias[j], w_out_b[j], k_past, v_past)
            k_p.append(kp)
            v_p.append(vp)
            k_s.append(kn)
            v_s.append(vn)
        xp = xp + yp
        xs = xs + ys
    return (xp, xs, jnp.stack(sg_p), jnp.stack(sg_s), jnp.stack(k_p), jnp.stack(v_p), jnp.stack(k_s), jnp.stack(v_s))
```

```python
import functools
import math

import jax
import jax.numpy as jnp
import numpy as np
from jax import lax
from jax.experimental import pallas as pl
from jax.experimental.pallas import tpu as pltpu

EPS = 1e-6
LOG2E = math.log2(math.e)

D_MODEL = 1024
GLA_HEADS = 4
GLA_DK = 128
GLA_DV = 256
GLA_KEY_W = GLA_HEADS * GLA_DK
GLA_VAL_W = GLA_HEADS * GLA_DV
GLA_RANK = 16
GLA_TAU = 16.0
GLA_CHUNK = 64
GLA_GROUP = 64
SB_HEADS = 16
SB_DH = 64
SB_W = SB_HEADS * SB_DH
PAGE = 128
LANES = 128
SUBLANES = 8
VMEM_LIMIT = 56 * 1024 * 1024

TQ = 256
TK = 128
Z2_MAX = 80.0

BF16 = jnp.bfloat16
F32 = jnp.float32


def _dot(a, b):
    return jnp.dot(a, b, preferred_element_type=F32)


def _dot_nt(a, b):
    return lax.dot_general(a, b, (((1,), (1,)), ((), ())), preferred_element_type=F32)


def _split2(x):
    hi = x.astype(BF16)
    lo = (x - hi.astype(F32)).astype(BF16)
    return hi, lo


def _log_sigmoid(x):
    return jnp.minimum(x, 0.0) - jnp.log(1.0 + jnp.exp(-jnp.abs(x)))


def _silu(x):
    return x / (1.0 + jnp.exp(-x))


def _rms_rows(x, gain):
    ms = jnp.mean(x * x, axis=-1, keepdims=True)
    return x * lax.rsqrt(ms + EPS) * gain


def _const_spec(shape):
    zeros = (0,) * len(shape)
    return pl.BlockSpec(shape, lambda *_: zeros, pipeline_mode=pl.Buffered(1))


def _gla_kernel(*refs, nb, tm, chunk, has_s0):
    if has_s0:
        (x_ref, s0_ref, gain_ref, wm_ref, wkt_ref, wac_ref, wact_ref, wup_ref, wupt_ref, ba_ref, bat_ref,
         on_ref, wout_ref, lcum_ref, mlater_ref, y_ref, sout_ref, s_scr, o_scr) = refs
    else:
        (x_ref, gain_ref, wm_ref, wkt_ref, wac_ref, wact_ref, wup_ref, wupt_ref, ba_ref, bat_ref,
         on_ref, wout_ref, lcum_ref, mlater_ref, y_ref, sout_ref, s_scr, o_scr) = refs
        s0_ref = None
    m = nb * tm
    step = pl.program_id(1)

    @pl.when(step == 0)
    def _():
        if has_s0:
            s_scr[...] = s0_ref[...]
        else:
            s_scr[...] = jnp.zeros_like(s_scr)

    x = x_ref[...].reshape(m, D_MODEL)
    h = _rms_rows(x, gain_ref[...]).astype(BF16)
    u = _dot(h, wm_ref[...])
    q = u[:, :GLA_KEY_W] * (GLA_DK ** -0.5)
    k = u[:, GLA_KEY_W:2 * GLA_KEY_W]
    v16 = u[:, 2 * GLA_KEY_W:2 * GLA_KEY_W + GLA_VAL_W].astype(BF16)
    gate = u[:, 2 * GLA_KEY_W + GLA_VAL_W:]

    code = _dot(h, wac_ref[...]).astype(BF16)
    la = _log_sigmoid(_dot(code, wup_ref[...]) + ba_ref[...]) * (1.0 / GLA_TAU)
    code_t = _dot_nt(wact_ref[...], h).astype(BF16)
    la_t = _log_sigmoid(_dot(wupt_ref[...], code_t) + bat_ref[...]) * (1.0 / GLA_TAU)
    k_t = _dot_nt(wkt_ref[...], h)

    la_hi, la_lo = _split2(la)
    b = _dot(lcum_ref[...], la_hi) + _dot(lcum_ref[...], la_lo)
    qd = (q * jnp.exp(b)).astype(BF16)
    kd = (k * jnp.exp(-b)).astype(BF16)
    lat_hi, lat_lo = _split2(la_t)
    rest_t = _dot(lat_hi, mlater_ref[...]) + _dot(lat_lo, mlater_ref[...])
    kl_t = k_t * jnp.exp(rest_t)

    lb = min(LANES, m)
    n_groups = m // GLA_GROUP
    n_sub = GLA_GROUP // chunk
    ri = lax.broadcasted_iota(jnp.int32, (GLA_GROUP, GLA_GROUP), 0)
    ci = lax.broadcasted_iota(jnp.int32, (GLA_GROUP, GLA_GROUP), 1)
    att_mask = ((ri // chunk) == (ci // chunk)) & (ci <= ri)
    ones_dv = jnp.ones((lb, GLA_DV), BF16)

    for g in range(n_groups):
        r0 = g * GLA_GROUP
        rows = slice(r0, r0 + GLA_GROUP)
        blk = (r0 // lb) * lb
        cols = slice(blk, blk + lb)
        for hd in range(GLA_HEADS):
            fk = slice(hd * GLA_DK, (hd + 1) * GLA_DK)
            fv = slice(hd * GLA_DV, (hd + 1) * GLA_DV)
            qd_g = qd[rows, fk]
            att = jnp.where(att_mask, _dot_nt(qd_g, kd[rows, fk]), 0.0).astype(BF16)
            o = _dot(att, v16[rows, fv])
            for j in range(n_sub):
                sub = (r0 + j * chunk) // chunk
                si = (r0 + j * chunk) // tm
                s_old = s_scr[si, hd]
                o_st = _dot(qd_g, s_old.astype(BF16))
                if n_sub > 1:
                    rsel = lax.broadcasted_iota(jnp.int32, (GLA_GROUP, GLA_DV), 0)
                    o_st = jnp.where((rsel + r0) // chunk == sub, o_st, 0.0)
                o = o + o_st
                csel = (lax.broadcasted_iota(jnp.int32, (GLA_DK, lb), 1) + blk) // chunk == sub
                la_sel = jnp.where(csel, la_t[fk, cols], 0.0)
                kl_sel = jnp.where(csel, kl_t[fk, cols], 0.0).astype(BF16)
                sel_hi, sel_lo = _split2(la_sel)
                decay = jnp.exp(_dot(sel_hi, ones_dv) + _dot(sel_lo, ones_dv))
                s_scr[si, hd] = s_old * decay + _dot(kl_sel, v16[cols, fv])
            o_scr[rows, fv] = o

    y = x
    for hd in range(GLA_HEADS):
        fv = slice(hd * GLA_DV, (hd + 1) * GLA_DV)
        on = _rms_rows(o_scr[:, fv], on_ref[...])
        gated = (on * _silu(gate[:, fv])).astype(BF16)
        y = y + _dot(gated, wout_ref[fv, :])
    y_ref[...] = y.reshape(nb, tm, D_MODEL)

    @pl.when(step == pl.num_programs(1) - 1)
    def _():
        sout_ref[...] = s_scr[...]


def _block_structure(m, chunk):
    idx = np.arange(m)
    same = (idx[:, None] // chunk) == (idx[None, :] // chunk)
    lcum = same & (idx[None, :] <= idx[:, None])
    later = same & (idx[:, None] > idx[None, :])
    return jnp.asarray(lcum, BF16), jnp.asarray(later, BF16)


def _gla_layer(x, s0, gain, w_in, w_up, b_a, onorm, w_out, *, nb, tm, chunk):
    bz, length, _ = x.shape
    has_s0 = s0 is not None
    m = nb * tm
    assert bz % nb == 0 and length % tm == 0 and m % GLA_GROUP == 0
    assert chunk == GLA_GROUP or (chunk == tm and m == GLA_GROUP)
    wm = w_in[:, :2 * GLA_KEY_W + 2 * GLA_VAL_W].astype(BF16)
    wkt = w_in[:, GLA_KEY_W:2 * GLA_KEY_W].T.astype(BF16)
    wac = jnp.pad(w_in[:, 2 * GLA_KEY_W + 2 * GLA_VAL_W:], ((0, 0), (0, LANES - GLA_RANK))).astype(BF16)
    wup = jnp.pad(w_up, ((0, LANES - GLA_RANK), (0, 0))).astype(BF16)
    lcum, later = _block_structure(m, chunk)
    consts = [gain.reshape(1, D_MODEL), wm, wkt, wac, wac.T, wup, wup.T, b_a.reshape(1, GLA_KEY_W),
              b_a.reshape(GLA_KEY_W, 1), onorm.reshape(1, GLA_DV), w_out.astype(BF16), lcum, later]
    state_shape = (nb, GLA_HEADS, GLA_DK, GLA_DV)
    state_spec = pl.BlockSpec(state_shape, lambda b, l: (b, 0, 0, 0))
    x_spec = pl.BlockSpec((nb, tm, D_MODEL), lambda b, l: (b, l, 0))
    in_specs = [x_spec] + ([state_spec] if has_s0 else []) + [_const_spec(c.shape) for c in consts]
    args = [x] + ([s0] if has_s0 else []) + consts
    return pl.pallas_call(
        functools.partial(_gla_kernel, nb=nb, tm=tm, chunk=chunk, has_s0=has_s0),
        grid=(bz // nb, length // tm),
        in_specs=in_specs,
        out_specs=[x_spec, state_spec],
        out_shape=[jax.ShapeDtypeStruct(x.shape, F32),
                   jax.ShapeDtypeStruct((bz, GLA_HEADS, GLA_DK, GLA_DV), F32)],
        scratch_shapes=[pltpu.VMEM(state_shape, F32), pltpu.VMEM((m, GLA_VAL_W), F32)],
        compiler_params=pltpu.CompilerParams(dimension_semantics=("arbitrary", "arbitrary"),
                                             vmem_limit_bytes=VMEM_LIMIT),
        name="gla_layer_s0" if has_s0 else "gla_layer",
    )(*args)


def _key_permutation():
    p = np.arange(TK)
    return 16 * (p % SUBLANES) + p // SUBLANES


def _head_norm_t(t, gain_col, tokens):
    t3 = t.reshape(SB_HEADS, SB_DH, tokens)
    ms = jnp.mean(t3 * t3, axis=1, keepdims=True)
    return (t3 * lax.rsqrt(ms + EPS)).reshape(SB_W, tokens) * gain_col


def _sb_in_kernel(*refs, tm, prompt):
    if prompt:
        (x_ref, gain_ref, wt_ref, qg_ref, kg_ref, p_ref, pt_ref,
         k_ref, v_ref, qt_ref, kp_ref, vtp_ref, gt_ref) = refs
    else:
        (x_ref, gain_ref, wt_ref, qg_ref, kg_ref, k_ref, v_ref, q_ref, g_ref) = refs
    x = x_ref[0]
    h = _rms_rows(x, gain_ref[...]).astype(BF16)
    u_t = _dot_nt(wt_ref[...], h)
    q_t = _head_norm_t(u_t[:SB_W], qg_ref[...], tm) * (SB_DH ** -0.5 * LOG2E)
    k_t = _head_norm_t(u_t[SB_W:2 * SB_W], kg_ref[...], tm)
    v_t = u_t[2 * SB_W:3 * SB_W]
    g_t = u_t[3 * SB_W:]
    if prompt:
        k_ref[0] = k_t
        v_ref[0] = v_t
        qt_ref[...] = q_t.astype(BF16).reshape(SB_HEADS, SB_DH, tm)
        gt_ref[...] = g_t
        kp_ref[...] = _dot_nt(p_ref[...], k_t.astype(BF16)).astype(BF16).reshape(tm // TK, TK, SB_W)
        vp_t = _dot(v_t.astype(BF16), pt_ref[...])
        for j in range(tm // TK):
            vtp_ref[:, j] = vp_t[:, j * TK:(j + 1) * TK].astype(BF16).reshape(SB_HEADS, SB_DH, TK)
    else:
        k_ref[0] = k_t.T
        v_ref[0] = v_t.T
        q_ref[0] = q_t.T
        g_ref[0] = g_t.T


def _sb_in(x, gain, w_in, qn, kn, *, tm, prompt):
    bz, length, _ = x.shape
    assert length % tm == 0 and tm % TK == 0
    nt = length // tm
    wt = w_in.T.astype(BF16)
    qg = jnp.tile(qn, SB_HEADS).reshape(SB_W, 1)
    kg = jnp.tile(kn, SB_HEADS).reshape(SB_W, 1)
    consts = [gain.reshape(1, D_MODEL), wt, qg, kg]
    nat_spec = pl.BlockSpec((1, tm, SB_W), lambda b, l: (b, l, 0))
    nat_shape = jax.ShapeDtypeStruct((bz, length, SB_W), F32)
    if prompt:
        assert tm == TQ
        perm = np.concatenate([_key_permutation() + j * TK for j in range(tm // TK)])
        pt = np.zeros((tm, tm), np.float32)
        pt[perm, np.arange(tm)] = 1.0
        consts += [jnp.asarray(pt.T, BF16), jnp.asarray(pt, BF16)]
        t_spec = pl.BlockSpec((1, SB_W, tm), lambda b, l: (b, 0, l))
        t_shape = jax.ShapeDtypeStruct((bz, SB_W, length), F32)
        out_specs = [
            t_spec, t_spec,
            pl.BlockSpec((None, SB_HEADS, None, SB_DH, tm), lambda b, l: (b, 0, l, 0, 0)),
            pl.BlockSpec((None, tm // TK, TK, SB_W), lambda b, l: (b, l, 0, 0)),
            pl.BlockSpec((None, SB_HEADS, tm // TK, SB_DH, TK), lambda b, l: (b, 0, l, 0, 0)),
            pl.BlockSpec((None, None, SB_W, tm), lambda b, l: (b, l, 0, 0)),
        ]
        out_shape = [
            t_shape, t_shape,
            jax.ShapeDtypeStruct((bz, SB_HEADS, nt, SB_DH, tm), BF16),
            jax.ShapeDtypeStruct((bz, length // TK, TK, SB_W), BF16),
            jax.ShapeDtypeStruct((bz, SB_HEADS, length // TK, SB_DH, TK), BF16),
            jax.ShapeDtypeStruct((bz, nt, SB_W, tm), F32),
        ]
    else:
        out_specs = [nat_spec] * 4
        out_shape = [nat_shape] * 4
    return pl.pallas_call(
        functools.partial(_sb_in_kernel, tm=tm, prompt=prompt),
        grid=(bz, nt),
        in_specs=[nat_spec] + [_const_spec(c.shape) for c in consts],
        out_specs=out_specs,
        out_shape=out_shape,
        compiler_params=pltpu.CompilerParams(dimension_semantics=("arbitrary", "arbitrary"),
                                             vmem_limit_bytes=VMEM_LIMIT),
        name="sb_in_prompt" if prompt else "sb_in_sample",
    )(x, *consts)


def _shift_rows_up(x, s, rid):
    return jnp.where(rid < SUBLANES - s, pltpu.roll(x, SUBLANES - s, axis=0), 1.0)


def _sb_tile(u, carry, rid, mask=None):
    n = u.shape[-1]
    w = jnp.exp2(jnp.minimum(u, Z2_MAX))
    if mask is not None:
        w = jnp.where(mask, w, 0.0)
    nv = TK // SUBLANES
    w3 = w.reshape(nv, SUBLANES, n)
    run = [None] * nv
    run[nv - 1] = 1.0 + w3[nv - 1]
    for i in range(nv - 2, -1, -1):
        run[i] = (1.0 + w3[i]) * run[i + 1]
    tot = run[0]
    later = _shift_rows_up(tot, 1, rid)
    later = later * _shift_rows_up(later, 1, rid)
    later = later * _shift_rows_up(later, 2, rid)
    later = later * _shift_rows_up(later, 4, rid)
    grp = later * carry
    a = jnp.concatenate([w3[i] / (run[i] * grp) for i in range(nv)], axis=0)
    new_carry = jnp.broadcast_to((grp * tot)[0:1], (SUBLANES, n))
    return a.astype(BF16), new_carry


def _sb_prompt_kernel(qt_ref, kp_ref, vtp_ref, bias_ref, ones_ref, ot_ref, *, nq):
    rid = lax.broadcasted_iota(jnp.int32, (SUBLANES, TQ), 0)
    prow = lax.broadcasted_iota(jnp.int32, (TK, TQ), 0)
    key_in_tile = 16 * (prow % SUBLANES) + prow // SUBLANES
    qcol = lax.broadcasted_iota(jnp.int32, (TK, TQ), 1)
    diag_masks = [(key_in_tile + t * TK) < qcol for t in range(TQ // TK)]
    zeros_q = jnp.zeros((SB_DH, TQ), BF16)
    ones_t = ones_ref[...]

    for h2 in range(2):
        def q_block(qb, _, h2=h2):
            q = qt_ref[h2, qb]
            parts = [q, zeros_q] if h2 == 0 else [zeros_q, q]
            qaug = jnp.concatenate(parts + [bias_ref[h2]], axis=0)

            def tile(kt, carry, acc, mask=None):
                lhs = jnp.concatenate([kp_ref[kt], ones_t], axis=1)
                a, carry = _sb_tile(_dot(lhs, qaug), carry, rid, mask)
                return carry, acc + _dot(vtp_ref[h2, kt], a)

            carry = jnp.ones((SUBLANES, TQ), F32)
            acc = jnp.zeros((SB_DH, TQ), F32)
            nd = TQ // TK
            for t in range(nd - 1, -1, -1):
                carry, acc = tile(qb * nd + t, carry, acc, diag_masks[t])

            def pair(j, st):
                carry, acc = st
                kt = qb * nd - 1 - 2 * j
                carry, acc = tile(kt, carry, acc)
                return tile(kt - 1, carry, acc)

            carry, acc = lax.fori_loop(0, qb * nd // 2, pair, (carry, acc))
            ot_ref[h2, qb] = acc
            return 0

        lax.fori_loop(0, nq, q_block, 0)


def _sb_prompt(qt, kp, vtp, bias):
    bz, _, nq, _, _ = qt.shape
    nk = kp.shape[1]
    b2 = bias.astype(F32) * LOG2E
    hi = b2.astype(BF16)
    mid = (b2 - hi.astype(F32)).astype(BF16)
    lo = (b2 - hi.astype(F32) - mid.astype(F32)).astype(BF16)
    rows = jnp.stack([hi, mid, lo], axis=1)
    bias_rows = jnp.zeros((SB_HEADS, LANES, TQ), BF16).at[:, :3, :].set(
        jnp.broadcast_to(rows[:, :, None], (SB_HEADS, 3, TQ)))
    bias_rows = bias_rows.reshape(SB_HEADS // 2, 2, LANES, TQ)
    ones_cols = jnp.zeros((TK, LANES), BF16).at[:, :3].set(1.0)
    pairs = SB_HEADS // 2
    return pl.pallas_call(
        functools.partial(_sb_prompt_kernel, nq=nq),
        grid=(bz * pairs,),
        in_specs=[
            pl.BlockSpec((None, 2, nq, SB_DH, TQ), lambda g: (g // pairs, g % pairs, 0, 0, 0)),
            pl.BlockSpec((None, nk, TK, LANES), lambda g: (g // pairs, 0, 0, g % pairs)),
            pl.BlockSpec((None, 2, nk, SB_DH, TK), lambda g: (g // pairs, g % pairs, 0, 0, 0)),
            pl.BlockSpec((None, 2, LANES, TQ), lambda g: (g % pairs, 0, 0, 0)),
            _const_spec((TK, LANES)),
        ],
        out_specs=pl.BlockSpec((None, 2, nq, SB_DH, TQ), lambda g: (g // pairs, g % pairs, 0, 0, 0)),
        out_shape=jax.ShapeDtypeStruct((bz, SB_HEADS, nq, SB_DH, TQ), F32),
        compiler_params=pltpu.CompilerParams(dimension_semantics=("arbitrary",),
                                             vmem_limit_bytes=VMEM_LIMIT),
        name="sb_prompt",
    )(qt, kp, vtp, bias_rows, ones_cols)


def _sb_out_t_kernel(ot_ref, gt_ref, x_ref, w_ref, y_ref):
    o_t = ot_ref[...].reshape(SB_W, TQ)
    gated = (o_t * _silu(gt_ref[...])).T.astype(BF16)
    y_ref[0] = x_ref[0] + _dot(gated, w_ref[...])


def _sb_out_t(ot, gt, x, w_out):
    bz, _, nq, _, _ = ot.shape
    x_spec = pl.BlockSpec((1, TQ, D_MODEL), lambda b, l: (b, l, 0))
    return pl.pallas_call(
        _sb_out_t_kernel,
        grid=(bz, nq),
        in_specs=[
            pl.BlockSpec((None, SB_HEADS, None, SB_DH, TQ), lambda b, l: (b, 0, l, 0, 0)),
            pl.BlockSpec((None, None, SB_W, TQ), lambda b, l: (b, l, 0, 0)),
            x_spec,
            _const_spec((SB_W, D_MODEL)),
        ],
        out_specs=x_spec,
        out_shape=jax.ShapeDtypeStruct(x.shape, F32),
        compiler_params=pltpu.CompilerParams(dimension_semantics=("arbitrary", "arbitrary"),
                                             vmem_limit_bytes=VMEM_LIMIT),
        name="sb_out_prompt",
    )(ot, gt, x, w_out.astype(BF16))


PAGES_PER_STEP = 8


def _rev_cumprod_rows(m, carry):
    rows, n = m.shape
    rid = lax.broadcasted_iota(jnp.int32, (rows, n), 0) % SUBLANES
    p = m
    for s in (1, 2, 4):
        p = p * jnp.where(rid < SUBLANES - s, pltpu.roll(p, rows - s, axis=0), 1.0)
    nv = rows // SUBLANES
    p3 = p.reshape(nv, SUBLANES, n)
    out = [None] * nv
    for i in range(nv - 1, -1, -1):
        out[i] = p3[i] * carry
        carry = jnp.broadcast_to(out[i][0:1], (SUBLANES, n))
    return jnp.concatenate(out, axis=0), carry


def _sb_sample_kernel(pt_ref, q_ref, kn_ref, vn_ref, bias_ref, *refs, seq, n_steps):
    kpages = refs[:PAGES_PER_STEP]
    vpages = refs[PAGES_PER_STEP:2 * PAGES_PER_STEP]
    o_ref = refs[2 * PAGES_PER_STEP]
    qbd_scr, carry_scr, acc_scr, kt16_scr = refs[2 * PAGES_PER_STEP + 1:]
    step = pl.program_id(1)
    hq = SB_HEADS * seq

    def attend(u, vt16, carry, mask=None):
        w = jnp.exp2(jnp.minimum(u + bias_ref[...], Z2_MAX))
        if mask is not None:
            w = jnp.where(mask, w, 0.0)
        prod, carry = _rev_cumprod_rows(1.0 + w, carry)
        return _dot(vt16, (w / prod).astype(BF16)), carry

    @pl.when(step == 0)
    def _():
        q = q_ref[...]
        qrow = jnp.concatenate([q] * SB_HEADS, axis=0)
        rh = lax.broadcasted_iota(jnp.int32, (hq, SB_W), 0) // seq
        lh = lax.broadcasted_iota(jnp.int32, (hq, SB_W), 1) // SB_DH
        qbd = jnp.where(rh == lh, qrow, 0.0).astype(BF16)
        qbd_scr[...] = qbd
        pad = jnp.zeros((PAGE - seq, SB_W), F32)
        k_new = jnp.concatenate([kn_ref[...], pad], axis=0).astype(BF16)
        vt_new = jnp.concatenate([vn_ref[...], pad], axis=0).T.astype(BF16)
        lane = lax.broadcasted_iota(jnp.int32, (PAGE, hq), 1)
        krow = lax.broadcasted_iota(jnp.int32, (PAGE, hq), 0)
        o, carry = attend(_dot_nt(k_new, qbd), vt_new, jnp.ones((SUBLANES, hq), F32), krow < (lane % seq))
        acc_scr[...] = o
        carry_scr[...] = carry

    for i in range(PAGES_PER_STEP):
        kt16_scr[:, i * PAGE:(i + 1) * PAGE] = kpages[i][...].astype(BF16)
    z_all = _dot(qbd_scr[...], kt16_scr[...])
    carry = carry_scr[...]
    acc = acc_scr[...]
    for i in range(PAGES_PER_STEP):
        o, carry = attend(z_all[:, i * PAGE:(i + 1) * PAGE].T, vpages[i][...].astype(BF16), carry)
        acc = acc + o
    acc_scr[...] = acc
    carry_scr[...] = carry

    @pl.when(step == n_steps - 1)
    def _():
        o3 = acc.T.reshape(SB_HEADS, seq, SB_W)
        hsel = lax.broadcasted_iota(jnp.int32, (SB_HEADS, seq, SB_W), 0)
        lsel = lax.broadcasted_iota(jnp.int32, (SB_HEADS, seq, SB_W), 2) // SB_DH
        o_ref[...] = jnp.sum(jnp.where(hsel == lsel, o3, 0.0), axis=0)


def _sb_sample(q, k_new, v_new, cache_kt, cache_vt, page_table, bias):
    bz, seq, _ = q.shape
    n_pages = page_table.shape[1]
    assert n_pages % PAGES_PER_STEP == 0 and seq == SUBLANES
    n_steps = n_pages // PAGES_PER_STEP
    bias_lane = jnp.repeat(bias.astype(F32) * LOG2E, seq).reshape(1, SB_HEADS * seq)
    tok_spec = pl.BlockSpec((None, seq, SB_W), lambda b, s, pt: (b, 0, 0))

    def page_spec(i):
        def index(b, s, pt):
            return (pt[b * n_pages + n_pages - 1 - (s * PAGES_PER_STEP + i)], 0, 0)
        return pl.BlockSpec((None, SB_W, PAGE), index)

    pages = [page_spec(i) for i in range(PAGES_PER_STEP)]
    hq = SB_HEADS * seq
    return pl.pallas_call(
        functools.partial(_sb_sample_kernel, seq=seq, n_steps=n_steps),
        grid_spec=pltpu.PrefetchScalarGridSpec(
            num_scalar_prefetch=1,
            grid=(bz, n_steps),
            in_specs=[tok_spec, tok_spec, tok_spec, pl.BlockSpec((1, hq), lambda b, s, pt: (0, 0))] + pages + pages,
            out_specs=tok_spec,
            scratch_shapes=[pltpu.VMEM((hq, SB_W), BF16), pltpu.VMEM((SUBLANES, hq), F32),
                            pltpu.VMEM((SB_W, hq), F32), pltpu.VMEM((SB_W, PAGES_PER_STEP * PAGE), BF16)],
        ),
        out_shape=jax.ShapeDtypeStruct((bz, seq, SB_W), F32),
        compiler_params=pltpu.CompilerParams(dimension_semantics=("arbitrary", "arbitrary"),
                                             vmem_limit_bytes=VMEM_LIMIT),
        name="sb_sample",
    )(page_table.reshape(-1), q, k_new, v_new, bias_lane,
      *([cache_kt] * PAGES_PER_STEP), *([cache_vt] * PAGES_PER_STEP))


def _sb_out_kernel(o_ref, g_ref, x_ref, w_ref, y_ref):
    gated = (o_ref[...] * _silu(g_ref[...])).astype(BF16)
    y_ref[...] = x_ref[...] + _dot(gated, w_ref[...])


def _sb_out(o, g, x, w_out, *, tm):
    n = o.shape[0]
    spec = pl.BlockSpec((tm, D_MODEL), lambda i: (i, 0))
    return pl.pallas_call(
        _sb_out_kernel,
        grid=(n // tm,),
        in_specs=[spec, spec, spec, _const_spec((SB_W, D_MODEL))],
        out_specs=spec,
        out_shape=jax.ShapeDtypeStruct((n, D_MODEL), F32),
        compiler_params=pltpu.CompilerParams(dimension_semantics=("arbitrary",), vmem_limit_bytes=VMEM_LIMIT),
        name="sb_out_sample",
    )(o, g, x, w_out.astype(BF16))


def kernel(x_prompt, x_sample, state_gla, cache_k, cache_v, page_table, norm_gain, w_in_a, w_alpha_up, b_alpha, onorm_a, w_out_a, w_in_b, qnorm_b, knorm_b, sb_bias, w_out_b):
    bz, seq_p, _ = x_prompt.shape
    dec_b, dec_s, _ = x_sample.shape

    gla = functools.partial(_gla_layer, gain=norm_gain[0], w_in=w_in_a[0], w_up=w_alpha_up[0], b_a=b_alpha[0],
                            onorm=onorm_a[0], w_out=w_out_a[0])
    xp, st_p = gla(x_prompt, None, nb=1, tm=4 * GLA_CHUNK, chunk=GLA_CHUNK)
    xs, st_s = gla(x_sample, state_gla[0], nb=GLA_GROUP // dec_s, tm=dec_s, chunk=dec_s)

    k_p, v_p, qt, kp, vtp, gt = _sb_in(xp, norm_gain[1], w_in_b[0], qnorm_b[0], knorm_b[0], tm=TQ, prompt=True)
    ot = _sb_prompt(qt, kp, vtp, sb_bias[0])
    yp = _sb_out_t(ot, gt, xp, w_out_b[0])

    xs_flat = xs.reshape(1, dec_b * dec_s, D_MODEL)
    k_s, v_s, q_s, g_s = _sb_in(xs_flat, norm_gain[1], w_in_b[0], qnorm_b[0], knorm_b[0], tm=TQ, prompt=False)
    per_seq = (dec_b, dec_s, SB_W)
    n_phys = cache_k.shape[1]
    cache_kt = jnp.transpose(cache_k[0], (0, 2, 3, 1)).reshape(n_phys, SB_W, PAGE)
    cache_vt = jnp.transpose(cache_v[0], (0, 2, 3, 1)).reshape(n_phys, SB_W, PAGE)
    o_s = _sb_sample(q_s.reshape(per_seq), k_s.reshape(per_seq), v_s.reshape(per_seq),
                     cache_kt, cache_vt, page_table, sb_bias[0])
    ys = _sb_out(o_s.reshape(dec_b * dec_s, SB_W), g_s[0], xs_flat[0], w_out_b[0], tm=TQ)

    heads = (SB_HEADS, SB_DH)

    def from_feature_major(t):
        return jnp.transpose(t.reshape(bz, *heads, seq_p), (0, 3, 1, 2))[None]

    return (yp, ys.reshape(dec_b, dec_s, D_MODEL), st_p[None], st_s[None],
            from_feature_major(k_p), from_feature_major(v_p),
            k_s.reshape(1, dec_b, dec_s, *heads), v_s.reshape(1, dec_b, dec_s, *heads))
```

```python
import functools
import math

import jax
import jax.numpy as jnp
import numpy as np
from jax import lax
from jax.experimental import pallas as pl
from jax.experimental.pallas import tpu as pltpu

EPS = 1e-6
LOG2E = math.log2(math.e)

D_MODEL = 1024
GLA_HEADS = 4
GLA_DK = 128
GLA_DV = 256
GLA_KEY_W = GLA_HEADS * GLA_DK
GLA_VAL_W = GLA_HEADS * GLA_DV
GLA_RANK = 16
GLA_TAU = 16.0
GLA_CHUNK = 64
GLA_GROUP = 64
SB_HEADS = 16
SB_DH = 64
SB_W = SB_HEADS * SB_DH
PAGE = 128
LANES = 128
SUBLANES = 8
VMEM_LIMIT = 56 * 1024 * 1024

TQ = 256
TK = 128
QB = 512
GK = 512
Z2_MAX = 80.0

BF16 = jnp.bfloat16
F32 = jnp.float32


def _dot(a, b):
    return jnp.dot(a, b, preferred_element_type=F32)


def _dot_nt(a, b):
    return lax.dot_general(a, b, (((1,), (1,)), ((), ())), preferred_element_type=F32)


def _split2(x):
    hi = x.astype(BF16)
    lo = (x - hi.astype(F32)).astype(BF16)
    return hi, lo


def _log_sigmoid(x):
    return jnp.minimum(x, 0.0) - jnp.log(1.0 + jnp.exp(-jnp.abs(x)))


def _silu(x):
    return x / (1.0 + jnp.exp(-x))


def _rms_rows(x, gain):
    ms = jnp.mean(x * x, axis=-1, keepdims=True)
    return x * lax.rsqrt(ms + EPS) * gain


def _const_spec(shape):
    zeros = (0,) * len(shape)
    return pl.BlockSpec(shape, lambda *_: zeros, pipeline_mode=pl.Buffered(1))


def _gla_kernel(*refs, nb, tm, chunk, has_s0):
    if has_s0:
        (x_ref, s0_ref, gain_ref, wm_ref, wkt_ref, wac_ref, wact_ref, wup_ref, wupt_ref, ba_ref, bat_ref,
         on_ref, wout_ref, lcum_ref, mlater_ref, y_ref, sout_ref, s_scr, o_scr) = refs
    else:
        (x_ref, gain_ref, wm_ref, wkt_ref, wac_ref, wact_ref, wup_ref, wupt_ref, ba_ref, bat_ref,
         on_ref, wout_ref, lcum_ref, mlater_ref, y_ref, sout_ref, s_scr, o_scr) = refs
        s0_ref = None
    m = nb * tm
    step = pl.program_id(1)

    @pl.when(step == 0)
    def _():
        if has_s0:
            s_scr[...] = s0_ref[...]
        else:
            s_scr[...] = jnp.zeros_like(s_scr)

    x = x_ref[...].reshape(m, D_MODEL)
    h = _rms_rows(x, gain_ref[...]).astype(BF16)
    u = _dot(h, wm_ref[...])
    q = u[:, :GLA_KEY_W] * (GLA_DK ** -0.5)
    k = u[:, GLA_KEY_W:2 * GLA_KEY_W]
    v16 = u[:, 2 * GLA_KEY_W:2 * GLA_KEY_W + GLA_VAL_W].astype(BF16)
    gate = u[:, 2 * GLA_KEY_W + GLA_VAL_W:]

    code = _dot(h, wac_ref[...]).astype(BF16)
    la = _log_sigmoid(_dot(code, wup_ref[...]) + ba_ref[...]) * (1.0 / GLA_TAU)
    code_t = _dot_nt(wact_ref[...], h).astype(BF16)
    la_t = _log_sigmoid(_dot(wupt_ref[...], code_t) + bat_ref[...]) * (1.0 / GLA_TAU)
    k_t = _dot_nt(wkt_ref[...], h)

    la_hi, la_lo = _split2(la)
    b = _dot(lcum_ref[...], la_hi) + _dot(lcum_ref[...], la_lo)
    qd = (q * jnp.exp(b)).astype(BF16)
    kd = (k * jnp.exp(-b)).astype(BF16)
    lat_hi, lat_lo = _split2(la_t)
    rest_t = _dot(lat_hi, mlater_ref[...]) + _dot(lat_lo, mlater_ref[...])
    kl_t = k_t * jnp.exp(rest_t)

    lb = min(LANES, m)
    n_groups = m // GLA_GROUP
    n_sub = GLA_GROUP // chunk
    ri = lax.broadcasted_iota(jnp.int32, (GLA_GROUP, GLA_GROUP), 0)
    ci = lax.broadcasted_iota(jnp.int32, (GLA_GROUP, GLA_GROUP), 1)
    att_mask = ((ri // chunk) == (ci // chunk)) & (ci <= ri)
    ones_dv = jnp.ones((lb, GLA_DV), BF16)

    for g in range(n_groups):
        r0 = g * GLA_GROUP
        rows = slice(r0, r0 + GLA_GROUP)
        blk = (r0 // lb) * lb
        cols = slice(blk, blk + lb)
        for hd in range(GLA_HEADS):
            fk = slice(hd * GLA_DK, (hd + 1) * GLA_DK)
            fv = slice(hd * GLA_DV, (hd + 1) * GLA_DV)
            qd_g = qd[rows, fk]
            att = jnp.where(att_mask, _dot_nt(qd_g, kd[rows, fk]), 0.0).astype(BF16)
            o = _dot(att, v16[rows, fv])
            for j in range(n_sub):
                sub = (r0 + j * chunk) // chunk
                si = (r0 + j * chunk) // tm
                s_old = s_scr[si, hd]
                o_st = _dot(qd_g, s_old.astype(BF16))
                if n_sub > 1:
                    rsel = lax.broadcasted_iota(jnp.int32, (GLA_GROUP, GLA_DV), 0)
                    o_st = jnp.where((rsel + r0) // chunk == sub, o_st, 0.0)
                o = o + o_st
                csel = (lax.broadcasted_iota(jnp.int32, (GLA_DK, lb), 1) + blk) // chunk == sub
                la_sel = jnp.where(csel, la_t[fk, cols], 0.0)
                kl_sel = jnp.where(csel, kl_t[fk, cols], 0.0).astype(BF16)
                sel_hi, sel_lo = _split2(la_sel)
                decay = jnp.exp(_dot(sel_hi, ones_dv) + _dot(sel_lo, ones_dv))
                s_scr[si, hd] = s_old * decay + _dot(kl_sel, v16[cols, fv])
            o_scr[rows, fv] = o

    y = x
    for hd in range(GLA_HEADS):
        fv = slice(hd * GLA_DV, (hd + 1) * GLA_DV)
        on = _rms_rows(o_scr[:, fv], on_ref[...])
        gated = (on * _silu(gate[:, fv])).astype(BF16)
        y = y + _dot(gated, wout_ref[fv, :])
    y_ref[...] = y.reshape(nb, tm, D_MODEL)

    @pl.when(step == pl.num_programs(1) - 1)
    def _():
        sout_ref[...] = s_scr[...]


def _block_structure(m, chunk):
    idx = np.arange(m)
    same = (idx[:, None] // chunk) == (idx[None, :] // chunk)
    lcum = same & (idx[None, :] <= idx[:, None])
    later = same & (idx[:, None] > idx[None, :])
    return jnp.asarray(lcum, BF16), jnp.asarray(later, BF16)


def _gla_layer(x, s0, gain, w_in, w_up, b_a, onorm, w_out, *, nb, tm, chunk):
    bz, length, _ = x.shape
    has_s0 = s0 is not None
    m = nb * tm
    assert bz % nb == 0 and length % tm == 0 and m % GLA_GROUP == 0
    assert chunk == GLA_GROUP or (chunk == tm and m == GLA_GROUP)
    wm = w_in[:, :2 * GLA_KEY_W + 2 * GLA_VAL_W].astype(BF16)
    wkt = w_in[:, GLA_KEY_W:2 * GLA_KEY_W].T.astype(BF16)
    wac = jnp.pad(w_in[:, 2 * GLA_KEY_W + 2 * GLA_VAL_W:], ((0, 0), (0, LANES - GLA_RANK))).astype(BF16)
    wup = jnp.pad(w_up, ((0, LANES - GLA_RANK), (0, 0))).astype(BF16)
    lcum, later = _block_structure(m, chunk)
    consts = [gain.reshape(1, D_MODEL), wm, wkt, wac, wac.T, wup, wup.T, b_a.reshape(1, GLA_KEY_W),
              b_a.reshape(GLA_KEY_W, 1), onorm.reshape(1, GLA_DV), w_out.astype(BF16), lcum, later]
    state_shape = (nb, GLA_HEADS, GLA_DK, GLA_DV)
    state_spec = pl.BlockSpec(state_shape, lambda b, l: (b, 0, 0, 0))
    x_spec = pl.BlockSpec((nb, tm, D_MODEL), lambda b, l: (b, l, 0))
    in_specs = [x_spec] + ([state_spec] if has_s0 else []) + [_const_spec(c.shape) for c in consts]
    args = [x] + ([s0] if has_s0 else []) + consts
    return pl.pallas_call(
        functools.partial(_gla_kernel, nb=nb, tm=tm, chunk=chunk, has_s0=has_s0),
        grid=(bz // nb, length // tm),
        in_specs=in_specs,
        out_specs=[x_spec, state_spec],
        out_shape=[jax.ShapeDtypeStruct(x.shape, F32),
                   jax.ShapeDtypeStruct((bz, GLA_HEADS, GLA_DK, GLA_DV), F32)],
        scratch_shapes=[pltpu.VMEM(state_shape, F32), pltpu.VMEM((m, GLA_VAL_W), F32)],
        compiler_params=pltpu.CompilerParams(dimension_semantics=("arbitrary", "arbitrary"),
                                             vmem_limit_bytes=VMEM_LIMIT),
        name="gla_layer_s0" if has_s0 else "gla_layer",
    )(*args)


def _key_permutation():
    p = np.arange(TK)
    return 16 * (p % SUBLANES) + p // SUBLANES


def _head_norm_t(t, gain_col, tokens):
    t3 = t.reshape(SB_HEADS, SB_DH, tokens)
    ms = jnp.mean(t3 * t3, axis=1, keepdims=True)
    return (t3 * lax.rsqrt(ms + EPS)).reshape(SB_W, tokens) * gain_col


def _sb_in_kernel(*refs, tm, prompt):
    if prompt:
        (x_ref, gain_ref, wt_ref, qg_ref, kg_ref, p_ref, pt_ref,
         k_ref, v_ref, qt_ref, kp_ref, vtp_ref, gt_ref) = refs
    else:
        (x_ref, gain_ref, wt_ref, qg_ref, kg_ref, k_ref, v_ref, q_ref, g_ref) = refs
    x = x_ref[0]
    h = _rms_rows(x, gain_ref[...]).astype(BF16)
    u_t = _dot_nt(wt_ref[...], h)
    q_t = _head_norm_t(u_t[:SB_W], qg_ref[...], tm) * (SB_DH ** -0.5 * LOG2E)
    k_t = _head_norm_t(u_t[SB_W:2 * SB_W], kg_ref[...], tm)
    v_t = u_t[2 * SB_W:3 * SB_W]
    g_t = u_t[3 * SB_W:]
    if prompt:
        k_ref[0] = k_t
        v_ref[0] = v_t
        qt_ref[...] = q_t.astype(BF16).reshape(SB_HEADS, SB_DH, tm)
        gt_ref[...] = g_t
        kp_ref[...] = _dot_nt(p_ref[...], k_t.astype(BF16)).astype(BF16).reshape(tm // TK, TK, SB_W)
        vp_t = _dot(v_t.astype(BF16), pt_ref[...])
        for j in range(tm // TK):
            vtp_ref[:, j] = vp_t[:, j * TK:(j + 1) * TK].astype(BF16).reshape(SB_HEADS, SB_DH, TK)
    else:
        k_ref[0] = k_t.T
        v_ref[0] = v_t.T
        q_ref[0] = q_t.T
        g_ref[0] = g_t.T


def _sb_in(x, gain, w_in, qn, kn, *, tm, prompt):
    bz, length, _ = x.shape
    assert length % tm == 0 and tm % TK == 0
    nt = length // tm
    wt = w_in.T.astype(BF16)
    qg = jnp.tile(qn, SB_HEADS).reshape(SB_W, 1)
    kg = jnp.tile(kn, SB_HEADS).reshape(SB_W, 1)
    consts = [gain.reshape(1, D_MODEL), wt, qg, kg]
    nat_spec = pl.BlockSpec((1, tm, SB_W), lambda b, l: (b, l, 0))
    nat_shape = jax.ShapeDtypeStruct((bz, length, SB_W), F32)
    if prompt:
        assert tm == TQ
        perm = np.concatenate([_key_permutation() + j * TK for j in range(tm // TK)])
        pt = np.zeros((tm, tm), np.float32)
        pt[perm, np.arange(tm)] = 1.0
        consts += [jnp.asarray(pt.T, BF16), jnp.asarray(pt, BF16)]
        t_spec = pl.BlockSpec((1, SB_W, tm), lambda b, l: (b, 0, l))
        t_shape = jax.ShapeDtypeStruct((bz, SB_W, length), F32)
        out_specs = [
            t_spec, t_spec,
            pl.BlockSpec((None, SB_HEADS, None, SB_DH, tm), lambda b, l: (b, 0, l, 0, 0)),
            pl.BlockSpec((None, tm // TK, TK, SB_W), lambda b, l: (b, l, 0, 0)),
            pl.BlockSpec((None, SB_HEADS, tm // TK, SB_DH, TK), lambda b, l: (b, 0, l, 0, 0)),
            pl.BlockSpec((None, None, SB_W, tm), lambda b, l: (b, l, 0, 0)),
        ]
        out_shape = [
            t_shape, t_shape,
            jax.ShapeDtypeStruct((bz, SB_HEADS, nt, SB_DH, tm), BF16),
            jax.ShapeDtypeStruct((bz, length // TK, TK, SB_W), BF16),
            jax.ShapeDtypeStruct((bz, SB_HEADS, length // TK, SB_DH, TK), BF16),
            jax.ShapeDtypeStruct((bz, nt, SB_W, tm), F32),
        ]
    else:
        out_specs = [nat_spec] * 4
        out_shape = [nat_shape] * 4
    return pl.pallas_call(
        functools.partial(_sb_in_kernel, tm=tm, prompt=prompt),
        grid=(bz, nt),
        in_specs=[nat_spec] + [_const_spec(c.shape) for c in consts],
        out_specs=out_specs,
        out_shape=out_shape,
        compiler_params=pltpu.CompilerParams(dimension_semantics=("arbitrary", "arbitrary"),
                                             vmem_limit_bytes=VMEM_LIMIT),
        name="sb_in_prompt" if prompt else "sb_in_sample",
    )(x, *consts)


def _shift_rows_up(x, s, rid):
    return jnp.where(rid < SUBLANES - s, pltpu.roll(x, SUBLANES - s, axis=0), 1.0)


def _sb_tile(u, carry, rid, mask=None):
    n = u.shape[-1]
    w = jnp.exp2(u)
    if mask is not None:
        w = jnp.where(mask, w, 0.0)
    nv = TK // SUBLANES
    m3 = (1.0 + w).reshape(nv, SUBLANES, n)
    run = [None] * nv
    run[nv - 1] = m3[nv - 1]
    for i in range(nv - 2, -1, -1):
        run[i] = m3[i] * run[i + 1]
    tot = run[0]
    later = _shift_rows_up(tot, 1, rid)
    later = later * _shift_rows_up(later, 1, rid)
    later = later * _shift_rows_up(later, 2, rid)
    later = later * _shift_rows_up(later, 4, rid)
    grp = later * carry
    inv = [1.0 / (run[i] * grp) for i in range(nv)] + [1.0 / grp]
    a = jnp.concatenate([inv[i + 1] - inv[i] for i in range(nv)], axis=0)
    new_carry = jnp.broadcast_to((grp * tot)[0:1], (SUBLANES, n))
    return a.astype(BF16), new_carry


def _sb_prompt_kernel(qt_ref, kp_ref, vtp_ref, bias_ref, ones_ref, ot_ref, u_scr, a_scr, *, nq):
    rid = lax.broadcasted_iota(jnp.int32, (SUBLANES, LANES), 0)
    prow = lax.broadcasted_iota(jnp.int32, (TK, LANES), 0)
    tri_mask = (16 * (prow % SUBLANES) + prow // SUBLANES) < lax.broadcasted_iota(jnp.int32, (TK, LANES), 1)
    zeros_q = jnp.zeros((SB_DH, QB), BF16)
    tiles = GK // TK
    halves = QB // TQ
    n_groups = nq * (nq + 1) // 2

    for h2 in range(2):
        def scores(qb, g, slot, h2=h2):
            q = jnp.concatenate([qt_ref[h2, halves * qb + i] for i in range(halves)], axis=1)
            parts = [q, zeros_q] if h2 == 0 else [zeros_q, q]
            qaug = jnp.concatenate(parts + [bias_ref[h2]], axis=0)
            keys = kp_ref[pl.ds(g * tiles, tiles)].reshape(GK, LANES)
            u_scr[slot] = _dot(jnp.concatenate([keys, ones_ref[...]], axis=1), qaug)

        def weights(slot, carry, diag):
            out = []
            for c in range(QB // LANES):
                cs = slice(c * LANES, (c + 1) * LANES)
                car = carry[:, cs]
                for t in range(tiles - 1, -1, -1):
                    rows = slice(t * TK, (t + 1) * TK)
                    if diag and t > c:
                        a_scr[slot, rows, cs] = jnp.zeros((TK, LANES), BF16)
                        continue
                    mask = tri_mask if diag and t == c else None
                    a, car = _sb_tile(u_scr[slot, rows, cs], car, rid, mask)
                    a_scr[slot, rows, cs] = a
                out.append(car)
            return jnp.concatenate(out, axis=1)

        def values(qb, g, slot, acc, h2=h2):
            vt = jnp.concatenate([vtp_ref[h2, g * tiles + t] for t in range(tiles)], axis=1)
            acc = acc + _dot(vt, a_scr[slot])
            for i in range(halves):
                ot_ref[h2, halves * qb + i] = acc[:, i * TQ:(i + 1) * TQ]
            return jnp.where(g == 0, 0.0, acc)

        def step(qb, g, pqb, pg, carry, acc, *, slot):
            last = g == 0
            nqb = jnp.minimum(jnp.where(last, qb + 1, qb), nq - 1)
            ng = jnp.where(last, nqb, g - 1)

            def body(diag):
                def run(carry, acc):
                    scores(nqb, ng, 1 - slot)
                    start = jnp.ones((SUBLANES, QB), F32) if diag else carry
                    return weights(slot, start, diag), values(pqb, pg, 1 - slot, acc)
                return run

            carry, acc = lax.cond(g == qb, body(True), body(False), carry, acc)
            return nqb, ng, qb, g, carry, acc

        zero = jnp.zeros((), jnp.int32)
        scores(zero, zero, 0)
        a_scr[1] = jnp.zeros((GK, QB), BF16)
        st = (zero, zero, zero, zero + 1, jnp.ones((SUBLANES, QB), F32), jnp.zeros((SB_DH, QB), F32))
        first = n_groups % 2
        if first:
            st = step(*st, slot=0)
        st = lax.fori_loop(0, n_groups // 2,
                           lambda _, s: step(*step(*s, slot=first), slot=1 - first), st)
        values(st[2], st[3], (n_groups - 1) % 2, st[5])


def _sb_prompt(qt, kp, vtp, bias):
    bz, _, nq_t, _, _ = qt.shape
    nk = kp.shape[1]
    assert (nq_t * TQ) % QB == 0
    nq = nq_t * TQ // QB
    b2 = bias.astype(F32) * LOG2E
    hi = b2.astype(BF16)
    mid = (b2 - hi.astype(F32)).astype(BF16)
    lo = (b2 - hi.astype(F32) - mid.astype(F32)).astype(BF16)
    rows = jnp.stack([hi, mid, lo], axis=1)
    bias_rows = jnp.zeros((SB_HEADS, LANES, QB), BF16).at[:, :3, :].set(
        jnp.broadcast_to(rows[:, :, None], (SB_HEADS, 3, QB)))
    bias_rows = bias_rows.reshape(SB_HEADS // 2, 2, LANES, QB)
    ones_cols = jnp.zeros((GK, LANES), BF16).at[:, :3].set(1.0)
    pairs = SB_HEADS // 2
    return pl.pallas_call(
        functools.partial(_sb_prompt_kernel, nq=nq),
        grid=(bz * pairs,),
        in_specs=[
            pl.BlockSpec((None, 2, nq_t, SB_DH, TQ), lambda g: (g // pairs, g % pairs, 0, 0, 0)),
            pl.BlockSpec((None, nk, TK, LANES), lambda g: (g // pairs, 0, 0, g % pairs)),
            pl.BlockSpec((None, 2, nk, SB_DH, TK), lambda g: (g // pairs, g % pairs, 0, 0, 0)),
            pl.BlockSpec((None, 2, LANES, QB), lambda g: (g % pairs, 0, 0, 0)),
            _const_spec((GK, LANES)),
        ],
        out_specs=pl.BlockSpec((None, 2, nq_t, SB_DH, TQ), lambda g: (g // pairs, g % pairs, 0, 0, 0)),
        out_shape=jax.ShapeDtypeStruct((bz, SB_HEADS, nq_t, SB_DH, TQ), F32),
        scratch_shapes=[pltpu.VMEM((2, GK, QB), F32), pltpu.VMEM((2, GK, QB), BF16)],
        compiler_params=pltpu.CompilerParams(dimension_semantics=("arbitrary",),
                                             vmem_limit_bytes=VMEM_LIMIT),
        name="sb_prompt",
    )(qt, kp, vtp, bias_rows, ones_cols)


def _sb_out_t_kernel(ot_ref, gt_ref, x_ref, w_ref, y_ref):
    o_t = ot_ref[...].reshape(SB_W, TQ)
    gated = (o_t * _silu(gt_ref[...])).T.astype(BF16)
    y_ref[0] = x_ref[0] + _dot(gated, w_ref[...])


def _sb_out_t(ot, gt, x, w_out):
    bz, _, nq, _, _ = ot.shape
    x_spec = pl.BlockSpec((1, TQ, D_MODEL), lambda b, l: (b, l, 0))
    return pl.pallas_call(
        _sb_out_t_kernel,
        grid=(bz, nq),
        in_specs=[
            pl.BlockSpec((None, SB_HEADS, None, SB_DH, TQ), lambda b, l: (b, 0, l, 0, 0)),
            pl.BlockSpec((None, None, SB_W, TQ), lambda b, l: (b, l, 0, 0)),
            x_spec,
            _const_spec((SB_W, D_MODEL)),
        ],
        out_specs=x_spec,
        out_shape=jax.ShapeDtypeStruct(x.shape, F32),
        compiler_params=pltpu.CompilerParams(dimension_semantics=("arbitrary", "arbitrary"),
                                             vmem_limit_bytes=VMEM_LIMIT),
        name="sb_out_prompt",
    )(ot, gt, x, w_out.astype(BF16))


PAGES_PER_STEP = 8


def _rev_cumprod_rows(m, carry):
    rows, n = m.shape
    rid = lax.broadcasted_iota(jnp.int32, (rows, n), 0) % SUBLANES
    p = m
    for s in (1, 2, 4):
        p = p * jnp.where(rid < SUBLANES - s, pltpu.roll(p, rows - s, axis=0), 1.0)
    nv = rows // SUBLANES
    p3 = p.reshape(nv, SUBLANES, n)
    out = [None] * nv
    for i in range(nv - 1, -1, -1):
        out[i] = p3[i] * carry
        carry = jnp.broadcast_to(out[i][0:1], (SUBLANES, n))
    return jnp.concatenate(out, axis=0), carry


def _sb_sample_kernel(pt_ref, q_ref, kn_ref, vn_ref, bias_ref, *refs, seq, n_steps):
    kpages = refs[:PAGES_PER_STEP]
    vpages = refs[PAGES_PER_STEP:2 * PAGES_PER_STEP]
    o_ref = refs[2 * PAGES_PER_STEP]
    qbd_scr, carry_scr, acc_scr, kt16_scr = refs[2 * PAGES_PER_STEP + 1:]
    step = pl.program_id(1)
    hq = SB_HEADS * seq

    def attend(u, vt16, carry, mask=None):
        w = jnp.exp2(jnp.minimum(u + bias_ref[...], Z2_MAX))
        if mask is not None:
            w = jnp.where(mask, w, 0.0)
        prod, carry = _rev_cumprod_rows(1.0 + w, carry)
        return _dot(vt16, (w / prod).astype(BF16)), carry

    @pl.when(step == 0)
    def _():
        q = q_ref[...]
        qrow = jnp.concatenate([q] * SB_HEADS, axis=0)
        rh = lax.broadcasted_iota(jnp.int32, (hq, SB_W), 0) // seq
        lh = lax.broadcasted_iota(jnp.int32, (hq, SB_W), 1) // SB_DH
        qbd = jnp.where(rh == lh, qrow, 0.0).astype(BF16)
        qbd_scr[...] = qbd
        pad = jnp.zeros((PAGE - seq, SB_W), F32)
        k_new = jnp.concatenate([kn_ref[...], pad], axis=0).astype(BF16)
        vt_new = jnp.concatenate([vn_ref[...], pad], axis=0).T.astype(BF16)
        lane = lax.broadcasted_iota(jnp.int32, (PAGE, hq), 1)
        krow = lax.broadcasted_iota(jnp.int32, (PAGE, hq), 0)
        o, carry = attend(_dot_nt(k_new, qbd), vt_new, jnp.ones((SUBLANES, hq), F32), krow < (lane % seq))
        acc_scr[...] = o
        carry_scr[...] = carry

    for i in range(PAGES_PER_STEP):
        kt16_scr[:, i * PAGE:(i + 1) * PAGE] = kpages[i][...].astype(BF16)
    z_all = _dot(qbd_scr[...], kt16_scr[...])
    carry = carry_scr[...]
    acc = acc_scr[...]
    for i in range(PAGES_PER_STEP):
        o, carry = attend(z_all[:, i * PAGE:(i + 1) * PAGE].T, vpages[i][...].astype(BF16), carry)
        acc = acc + o
    acc_scr[...] = acc
    carry_scr[...] = carry

    @pl.when(step == n_steps - 1)
    def _():
        o3 = acc.T.reshape(SB_HEADS, seq, SB_W)
        hsel = lax.broadcasted_iota(jnp.int32, (SB_HEADS, seq, SB_W), 0)
        lsel = lax.broadcasted_iota(jnp.int32, (SB_HEADS, seq, SB_W), 2) // SB_DH
        o_ref[...] = jnp.sum(jnp.where(hsel == lsel, o3, 0.0), axis=0)


def _sb_sample(q, k_new, v_new, cache_kt, cache_vt, page_table, bias):
    bz, seq, _ = q.shape
    n_pages = page_table.shape[1]
    assert n_pages % PAGES_PER_STEP == 0 and seq == SUBLANES
    n_steps = n_pages // PAGES_PER_STEP
    bias_lane = jnp.repeat(bias.astype(F32) * LOG2E, seq).reshape(1, SB_HEADS * seq)
    tok_spec = pl.BlockSpec((None, seq, SB_W), lambda b, s, pt: (b, 0, 0))

    def page_spec(i):
        def index(b, s, pt):
            return (pt[b * n_pages + n_pages - 1 - (s * PAGES_PER_STEP + i)], 0, 0)
        return pl.BlockSpec((None, SB_W, PAGE), index)

    pages = [page_spec(i) for i in range(PAGES_PER_STEP)]
    hq = SB_HEADS * seq
    return pl.pallas_call(
        functools.partial(_sb_sample_kernel, seq=seq, n_steps=n_steps),
        grid_spec=pltpu.PrefetchScalarGridSpec(
            num_scalar_prefetch=1,
            grid=(bz, n_steps),
            in_specs=[tok_spec, tok_spec, tok_spec, pl.BlockSpec((1, hq), lambda b, s, pt: (0, 0))] + pages + pages,
            out_specs=tok_spec,
            scratch_shapes=[pltpu.VMEM((hq, SB_W), BF16), pltpu.VMEM((SUBLANES, hq), F32),
                            pltpu.VMEM((SB_W, hq), F32), pltpu.VMEM((SB_W, PAGES_PER_STEP * PAGE), BF16)],
        ),
        out_shape=jax.ShapeDtypeStruct((bz, seq, SB_W), F32),
        compiler_params=pltpu.CompilerParams(dimension_semantics=("arbitrary", "arbitrary"),
                                             vmem_limit_bytes=VMEM_LIMIT),
        name="sb_sample",
    )(page_table.reshape(-1), q, k_new, v_new, bias_lane,
      *([cache_kt] * PAGES_PER_STEP), *([cache_vt] * PAGES_PER_STEP))


def _sb_out_kernel(o_ref, g_ref, x_ref, w_ref, y_ref):
    gated = (o_ref[...] * _silu(g_ref[...])).astype(BF16)
    y_ref[...] = x_ref[...] + _dot(gated, w_ref[...])


def _sb_out(o, g, x, w_out, *, tm):
    n = o.shape[0]
    spec = pl.BlockSpec((tm, D_MODEL), lambda i: (i, 0))
    return pl.pallas_call(
        _sb_out_kernel,
        grid=(n // tm,),
        in_specs=[spec, spec, spec, _const_spec((SB_W, D_MODEL))],
        out_specs=spec,
        out_shape=jax.ShapeDtypeStruct((n, D_MODEL), F32),
        compiler_params=pltpu.CompilerParams(dimension_semantics=("arbitrary",), vmem_limit_bytes=VMEM_LIMIT),
        name="sb_out_sample",
    )(o, g, x, w_out.astype(BF16))


def kernel(x_prompt, x_sample, state_gla, cache_k, cache_v, page_table, norm_gain, w_in_a, w_alpha_up, b_alpha, onorm_a, w_out_a, w_in_b, qnorm_b, knorm_b, sb_bias, w_out_b):
    bz, seq_p, _ = x_prompt.shape
    dec_b, dec_s, _ = x_sample.shape

    gla = functools.partial(_gla_layer, gain=norm_gain[0], w_in=w_in_a[0], w_up=w_alpha_up[0], b_a=b_alpha[0],
                            onorm=onorm_a[0], w_out=w_out_a[0])
    xp, st_p = gla(x_prompt, None, nb=1, tm=4 * GLA_CHUNK, chunk=GLA_CHUNK)
    xs, st_s = gla(x_sample, state_gla[0], nb=GLA_GROUP // dec_s, tm=dec_s, chunk=dec_s)

    k_p, v_p, qt, kp, vtp, gt = _sb_in(xp, norm_gain[1], w_in_b[0], qnorm_b[0], knorm_b[0], tm=TQ, prompt=True)
    ot = _sb_prompt(qt, kp, vtp, sb_bias[0])
    yp = _sb_out_t(ot, gt, xp, w_out_b[0])

    xs_flat = xs.reshape(1, dec_b * dec_s, D_MODEL)
    k_s, v_s, q_s, g_s = _sb_in(xs_flat, norm_gain[1], w_in_b[0], qnorm_b[0], knorm_b[0], tm=TQ, prompt=False)
    per_seq = (dec_b, dec_s, SB_W)
    n_phys = cache_k.shape[1]
    cache_kt = jnp.transpose(cache_k[0], (0, 2, 3, 1)).reshape(n_phys, SB_W, PAGE)
    cache_vt = jnp.transpose(cache_v[0], (0, 2, 3, 1)).reshape(n_phys, SB_W, PAGE)
    o_s = _sb_sample(q_s.reshape(per_seq), k_s.reshape(per_seq), v_s.reshape(per_seq),
                     cache_kt, cache_vt, page_table, sb_bias[0])
    ys = _sb_out(o_s.reshape(dec_b * dec_s, SB_W), g_s[0], xs_flat[0], w_out_b[0], tm=TQ)

    heads = (SB_HEADS, SB_DH)

    def from_feature_major(t):
        return jnp.transpose(t.reshape(bz, *heads, seq_p), (0, 3, 1, 2))[None]

    return (yp, ys.reshape(dec_b, dec_s, D_MODEL), st_p[None], st_s[None],
            from_feature_major(k_p), from_feature_major(v_p),
            k_s.reshape(1, dec_b, dec_s, *heads), v_s.reshape(1, dec_b, dec_s, *heads))
```

```python
import functools
import math

import jax
import jax.numpy as jnp
import numpy as np
from jax import lax
from jax.experimental import pallas as pl
from jax.experimental.pallas import tpu as pltpu

EPS = 1e-6
LOG2E = math.log2(math.e)

D_MODEL = 1024
GLA_HEADS = 4
GLA_DK = 128
GLA_DV = 256
GLA_KEY_W = GLA_HEADS * GLA_DK
GLA_VAL_W = GLA_HEADS * GLA_DV
GLA_RANK = 16
GLA_TAU = 16.0
GLA_CHUNK = 64
GLA_GROUP = 64
SB_HEADS = 16
SB_DH = 64
SB_W = SB_HEADS * SB_DH
PAGE = 128
LANES = 128
SUBLANES = 8
VMEM_LIMIT = 56 * 1024 * 1024

TQ = 256
TK = 128
QB = 512
GK = 512
Z2_MAX = 80.0

BF16 = jnp.bfloat16
F32 = jnp.float32


def _dot(a, b):
    return jnp.dot(a, b, preferred_element_type=F32)


def _dot_nt(a, b):
    return lax.dot_general(a, b, (((1,), (1,)), ((), ())), preferred_element_type=F32)


def _split2(x):
    hi = x.astype(BF16)
    lo = (x - hi.astype(F32)).astype(BF16)
    return hi, lo


def _log_sigmoid(x):
    return jnp.minimum(x, 0.0) - jnp.log(1.0 + jnp.exp(-jnp.abs(x)))


def _silu(x):
    return x / (1.0 + jnp.exp(-x))


def _rms_rows(x, gain):
    ms = jnp.mean(x * x, axis=-1, keepdims=True)
    return x * lax.rsqrt(ms + EPS) * gain


def _const_spec(shape):
    zeros = (0,) * len(shape)
    return pl.BlockSpec(shape, lambda *_: zeros, pipeline_mode=pl.Buffered(1))


def _gla_kernel(*refs, nb, tm, chunk, has_s0):
    if has_s0:
        (x_ref, s0_ref, gain_ref, wm_ref, wkt_ref, wac_ref, wact_ref, wup_ref, wupt_ref, ba_ref, bat_ref,
         on_ref, wout_ref, lcum_ref, mlater_ref, y_ref, sout_ref, s_scr, o_scr) = refs
    else:
        (x_ref, gain_ref, wm_ref, wkt_ref, wac_ref, wact_ref, wup_ref, wupt_ref, ba_ref, bat_ref,
         on_ref, wout_ref, lcum_ref, mlater_ref, y_ref, sout_ref, s_scr, o_scr) = refs
        s0_ref = None
    m = nb * tm
    step = pl.program_id(1)

    @pl.when(step == 0)
    def _():
        if has_s0:
            s_scr[...] = s0_ref[...]
        else:
            s_scr[...] = jnp.zeros_like(s_scr)

    x = x_ref[...].reshape(m, D_MODEL)
    h = _rms_rows(x, gain_ref[...]).astype(BF16)
    u = _dot(h, wm_ref[...])
    q = u[:, :GLA_KEY_W] * (GLA_DK ** -0.5)
    k = u[:, GLA_KEY_W:2 * GLA_KEY_W]
    v16 = u[:, 2 * GLA_KEY_W:2 * GLA_KEY_W + GLA_VAL_W].astype(BF16)
    gate = u[:, 2 * GLA_KEY_W + GLA_VAL_W:]

    code = _dot(h, wac_ref[...]).astype(BF16)
    la = _log_sigmoid(_dot(code, wup_ref[...]) + ba_ref[...]) * (1.0 / GLA_TAU)
    code_t = _dot_nt(wact_ref[...], h).astype(BF16)
    la_t = _log_sigmoid(_dot(wupt_ref[...], code_t) + bat_ref[...]) * (1.0 / GLA_TAU)
    k_t = _dot_nt(wkt_ref[...], h)

    la_hi, la_lo = _split2(la)
    b = _dot(lcum_ref[...], la_hi) + _dot(lcum_ref[...], la_lo)
    qd = (q * jnp.exp(b)).astype(BF16)
    kd = (k * jnp.exp(-b)).astype(BF16)
    lat_hi, lat_lo = _split2(la_t)
    rest_t = _dot(lat_hi, mlater_ref[...]) + _dot(lat_lo, mlater_ref[...])
    kl_t = k_t * jnp.exp(rest_t)

    lb = min(LANES, m)
    n_groups = m // GLA_GROUP
    n_sub = GLA_GROUP // chunk
    ri = lax.broadcasted_iota(jnp.int32, (GLA_GROUP, GLA_GROUP), 0)
    ci = lax.broadcasted_iota(jnp.int32, (GLA_GROUP, GLA_GROUP), 1)
    att_mask = ((ri // chunk) == (ci // chunk)) & (ci <= ri)
    ones_dv = jnp.ones((lb, GLA_DV), BF16)

    for g in range(n_groups):
        r0 = g * GLA_GROUP
        rows = slice(r0, r0 + GLA_GROUP)
        blk = (r0 // lb) * lb
        cols = slice(blk, blk + lb)
        for hd in range(GLA_HEADS):
            fk = slice(hd * GLA_DK, (hd + 1) * GLA_DK)
            fv = slice(hd * GLA_DV, (hd + 1) * GLA_DV)
            qd_g = qd[rows, fk]
            att = jnp.where(att_mask, _dot_nt(qd_g, kd[rows, fk]), 0.0).astype(BF16)
            o = _dot(att, v16[rows, fv])
            for j in range(n_sub):
                sub = (r0 + j * chunk) // chunk
                si = (r0 + j * chunk) // tm
                s_old = s_scr[si, hd]
                o_st = _dot(qd_g, s_old.astype(BF16))
                if n_sub > 1:
                    rsel = lax.broadcasted_iota(jnp.int32, (GLA_GROUP, GLA_DV), 0)
                    o_st = jnp.where((rsel + r0) // chunk == sub, o_st, 0.0)
                o = o + o_st
                csel = (lax.broadcasted_iota(jnp.int32, (GLA_DK, lb), 1) + blk) // chunk == sub
                la_sel = jnp.where(csel, la_t[fk, cols], 0.0)
                kl_sel = jnp.where(csel, kl_t[fk, cols], 0.0).astype(BF16)
                sel_hi, sel_lo = _split2(la_sel)
                decay = jnp.exp(_dot(sel_hi, ones_dv) + _dot(sel_lo, ones_dv))
                s_scr[si, hd] = s_old * decay + _dot(kl_sel, v16[cols, fv])
            o_scr[rows, fv] = o

    y = x
    for hd in range(GLA_HEADS):
        fv = slice(hd * GLA_DV, (hd + 1) * GLA_DV)
        on = _rms_rows(o_scr[:, fv], on_ref[...])
        gated = (on * _silu(gate[:, fv])).astype(BF16)
        y = y + _dot(gated, wout_ref[fv, :])
    y_ref[...] = y.reshape(nb, tm, D_MODEL)

    @pl.when(step == pl.num_programs(1) - 1)
    def _():
        sout_ref[...] = s_scr[...]


def _block_structure(m, chunk):
    idx = np.arange(m)
    same = (idx[:, None] // chunk) == (idx[None, :] // chunk)
    lcum = same & (idx[None, :] <= idx[:, None])
    later = same & (idx[:, None] > idx[None, :])
    return jnp.asarray(lcum, BF16), jnp.asarray(later, BF16)


def _gla_layer(x, s0, gain, w_in, w_up, b_a, onorm, w_out, *, nb, tm, chunk):
    bz, length, _ = x.shape
    has_s0 = s0 is not None
    m = nb * tm
    assert bz % nb == 0 and length % tm == 0 and m % GLA_GROUP == 0
    assert chunk == GLA_GROUP or (chunk == tm and m == GLA_GROUP)
    wm = w_in[:, :2 * GLA_KEY_W + 2 * GLA_VAL_W].astype(BF16)
    wkt = w_in[:, GLA_KEY_W:2 * GLA_KEY_W].T.astype(BF16)
    wac = jnp.pad(w_in[:, 2 * GLA_KEY_W + 2 * GLA_VAL_W:], ((0, 0), (0, LANES - GLA_RANK))).astype(BF16)
    wup = jnp.pad(w_up, ((0, LANES - GLA_RANK), (0, 0))).astype(BF16)
    lcum, later = _block_structure(m, chunk)
    consts = [gain.reshape(1, D_MODEL), wm, wkt, wac, wac.T, wup, wup.T, b_a.reshape(1, GLA_KEY_W),
              b_a.reshape(GLA_KEY_W, 1), onorm.reshape(1, GLA_DV), w_out.astype(BF16), lcum, later]
    state_shape = (nb, GLA_HEADS, GLA_DK, GLA_DV)
    state_spec = pl.BlockSpec(state_shape, lambda b, l: (b, 0, 0, 0))
    x_spec = pl.BlockSpec((nb, tm, D_MODEL), lambda b, l: (b, l, 0))
    in_specs = [x_spec] + ([state_spec] if has_s0 else []) + [_const_spec(c.shape) for c in consts]
    args = [x] + ([s0] if has_s0 else []) + consts
    return pl.pallas_call(
        functools.partial(_gla_kernel, nb=nb, tm=tm, chunk=chunk, has_s0=has_s0),
        grid=(bz // nb, length // tm),
        in_specs=in_specs,
        out_specs=[x_spec, state_spec],
        out_shape=[jax.ShapeDtypeStruct(x.shape, F32),
                   jax.ShapeDtypeStruct((bz, GLA_HEADS, GLA_DK, GLA_DV), F32)],
        scratch_shapes=[pltpu.VMEM(state_shape, F32), pltpu.VMEM((m, GLA_VAL_W), F32)],
        compiler_params=pltpu.CompilerParams(dimension_semantics=("arbitrary", "arbitrary"),
                                             vmem_limit_bytes=VMEM_LIMIT),
        name="gla_layer_s0" if has_s0 else "gla_layer",
    )(*args)


def _key_permutation():
    p = np.arange(TK)
    return 16 * (p % SUBLANES) + p // SUBLANES


def _head_norm_t(t, gain_col, tokens):
    t3 = t.reshape(SB_HEADS, SB_DH, tokens)
    ms = jnp.mean(t3 * t3, axis=1, keepdims=True)
    return (t3 * lax.rsqrt(ms + EPS)).reshape(SB_W, tokens) * gain_col


def _sb_in_kernel(*refs, tm, prompt):
    if prompt:
        (x_ref, gain_ref, wt_ref, qg_ref, kg_ref, p_ref, pt_ref,
         k_ref, v_ref, qt_ref, kp_ref, vtp_ref, gt_ref) = refs
    else:
        (x_ref, gain_ref, wt_ref, qg_ref, kg_ref, k_ref, v_ref, q_ref, g_ref) = refs
    x = x_ref[0]
    h = _rms_rows(x, gain_ref[...]).astype(BF16)
    u_t = _dot_nt(wt_ref[...], h)
    q_t = _head_norm_t(u_t[:SB_W], qg_ref[...], tm) * (SB_DH ** -0.5 * (0.5 if prompt else LOG2E))
    k_t = _head_norm_t(u_t[SB_W:2 * SB_W], kg_ref[...], tm)
    v_t = u_t[2 * SB_W:3 * SB_W]
    g_t = u_t[3 * SB_W:]
    if prompt:
        k_ref[0] = k_t
        v_ref[0] = v_t
        qt_ref[...] = q_t.astype(BF16).reshape(SB_HEADS, SB_DH, tm)
        gt_ref[...] = g_t
        kp_ref[...] = _dot_nt(p_ref[...], k_t.astype(BF16)).astype(BF16).reshape(tm // TK, TK, SB_W)
        vp_t = _dot(v_t.astype(BF16), pt_ref[...])
        for j in range(tm // TK):
            vtp_ref[:, j] = vp_t[:, j * TK:(j + 1) * TK].astype(BF16).reshape(SB_HEADS, SB_DH, TK)
    else:
        k_ref[0] = k_t.T
        v_ref[0] = v_t.T
        q_ref[0] = q_t.T
        g_ref[0] = g_t.T


def _sb_in(x, gain, w_in, qn, kn, *, tm, prompt):
    bz, length, _ = x.shape
    assert length % tm == 0 and tm % TK == 0
    nt = length // tm
    wt = w_in.T.astype(BF16)
    qg = jnp.tile(qn, SB_HEADS).reshape(SB_W, 1)
    kg = jnp.tile(kn, SB_HEADS).reshape(SB_W, 1)
    consts = [gain.reshape(1, D_MODEL), wt, qg, kg]
    nat_spec = pl.BlockSpec((1, tm, SB_W), lambda b, l: (b, l, 0))
    nat_shape = jax.ShapeDtypeStruct((bz, length, SB_W), F32)
    if prompt:
        assert tm == TQ
        perm = np.concatenate([_key_permutation() + j * TK for j in range(tm // TK)])
        pt = np.zeros((tm, tm), np.float32)
        pt[perm, np.arange(tm)] = 1.0
        consts += [jnp.asarray(pt.T, BF16), jnp.asarray(pt, BF16)]
        t_spec = pl.BlockSpec((1, SB_W, tm), lambda b, l: (b, 0, l))
        t_shape = jax.ShapeDtypeStruct((bz, SB_W, length), F32)
        out_specs = [
            t_spec, t_spec,
            pl.BlockSpec((None, SB_HEADS, None, SB_DH, tm), lambda b, l: (b, 0, l, 0, 0)),
            pl.BlockSpec((None, tm // TK, TK, SB_W), lambda b, l: (b, l, 0, 0)),
            pl.BlockSpec((None, SB_HEADS, tm // TK, SB_DH, TK), lambda b, l: (b, 0, l, 0, 0)),
            pl.BlockSpec((None, None, SB_W, tm), lambda b, l: (b, l, 0, 0)),
        ]
        out_shape = [
            t_shape, t_shape,
            jax.ShapeDtypeStruct((bz, SB_HEADS, nt, SB_DH, tm), BF16),
            jax.ShapeDtypeStruct((bz, length // TK, TK, SB_W), BF16),
            jax.ShapeDtypeStruct((bz, SB_HEADS, length // TK, SB_DH, TK), BF16),
            jax.ShapeDtypeStruct((bz, nt, SB_W, tm), F32),
        ]
    else:
        out_specs = [nat_spec] * 4
        out_shape = [nat_shape] * 4
    return pl.pallas_call(
        functools.partial(_sb_in_kernel, tm=tm, prompt=prompt),
        grid=(bz, nt),
        in_specs=[nat_spec] + [_const_spec(c.shape) for c in consts],
        out_specs=out_specs,
        out_shape=out_shape,
        compiler_params=pltpu.CompilerParams(dimension_semantics=("arbitrary", "arbitrary"),
                                             vmem_limit_bytes=VMEM_LIMIT),
        name="sb_in_prompt" if prompt else "sb_in_sample",
    )(x, *consts)


def _shift_rows_up(x, s, rid):
    return jnp.where(rid < SUBLANES - s, pltpu.roll(x, SUBLANES - s, axis=0), 1.0)


def _sb_tile(u, carry, rid, mask=None):
    n = u.shape[-1]
    rho = 0.5 - 0.5 * jnp.tanh(u)
    if mask is not None:
        rho = jnp.where(mask, rho, 1.0)
    nv = TK // SUBLANES
    r3 = rho.reshape(nv, SUBLANES, n)
    run = [None] * nv
    run[nv - 1] = r3[nv - 1]
    for i in range(nv - 2, -1, -1):
        run[i] = r3[i] * run[i + 1]
    tot = run[0]
    later = _shift_rows_up(tot, 1, rid)
    later = later * _shift_rows_up(later, 1, rid)
    later = later * _shift_rows_up(later, 2, rid)
    later = later * _shift_rows_up(later, 4, rid)
    grp = later * carry
    prod = [run[i] * grp for i in range(nv)] + [grp]
    a = jnp.concatenate([prod[i + 1] - prod[i] for i in range(nv)], axis=0)
    new_carry = jnp.broadcast_to((grp * tot)[0:1], (SUBLANES, n))
    return a.astype(BF16), new_carry


def _sb_prompt_kernel(qt_ref, kp_ref, vtp_ref, bias_ref, ones_ref, ot_ref, u_scr, a_scr, *, nq):
    rid = lax.broadcasted_iota(jnp.int32, (SUBLANES, LANES), 0)
    prow = lax.broadcasted_iota(jnp.int32, (TK, LANES), 0)
    tri_mask = (16 * (prow % SUBLANES) + prow // SUBLANES) < lax.broadcasted_iota(jnp.int32, (TK, LANES), 1)
    zeros_q = jnp.zeros((SB_DH, QB), BF16)
    tiles = GK // TK
    halves = QB // TQ
    n_groups = nq * (nq + 1) // 2

    heads = range(2)

    def scores(h2, qb, g, slot):
        q = jnp.concatenate([qt_ref[h2, halves * qb + i] for i in range(halves)], axis=1)
        parts = [q, zeros_q] if h2 == 0 else [zeros_q, q]
        qaug = jnp.concatenate(parts + [bias_ref[h2]], axis=0)
        keys = kp_ref[pl.ds(g * tiles, tiles)].reshape(GK, LANES)
        u_scr[slot, h2] = _dot(jnp.concatenate([keys, ones_ref[...]], axis=1), qaug)

    def weights(h2, slot, carry, diag):
        out = []
        for c in range(QB // LANES):
            cs = slice(c * LANES, (c + 1) * LANES)
            car = carry[:, cs]
            for t in range(tiles - 1, -1, -1):
                rows = slice(t * TK, (t + 1) * TK)
                if diag and t > c:
                    a_scr[slot, h2, rows, cs] = jnp.zeros((TK, LANES), BF16)
                    continue
                mask = tri_mask if diag and t == c else None
                a, car = _sb_tile(u_scr[slot, h2, rows, cs], car, rid, mask)
                a_scr[slot, h2, rows, cs] = a
            out.append(car)
        return jnp.concatenate(out, axis=1)

    def values(h2, qb, g, slot):
        vt = jnp.concatenate([vtp_ref[h2, g * tiles + t] for t in range(tiles)], axis=1)
        part = _dot(vt, a_scr[slot, h2])
        for i in range(halves):
            prev = jnp.where(g == qb, 0.0, ot_ref[h2, halves * qb + i])
            ot_ref[h2, halves * qb + i] = prev + part[:, i * TQ:(i + 1) * TQ]

    def step(qb, g, pqb, pg, carries, *, slot):
        last = g == 0
        nqb = jnp.minimum(jnp.where(last, qb + 1, qb), nq - 1)
        ng = jnp.where(last, nqb, g - 1)

        def body(diag):
            def run(carries):
                ones = jnp.ones((SUBLANES, QB), F32)
                carries = tuple(weights(h2, slot, ones if diag else carries[h2], diag) for h2 in heads)
                for h2 in heads:
                    values(h2, pqb, pg, 1 - slot)
                    scores(h2, nqb, ng, 1 - slot)
                return carries
            return run

        return nqb, ng, qb, g, lax.cond(g == qb, body(True), body(False), carries)

    zero = jnp.zeros((), jnp.int32)
    for h2 in heads:
        scores(h2, zero, zero, 0)
    a_scr[1] = jnp.zeros((2, GK, QB), BF16)
    ot_ref[...] = jnp.zeros_like(ot_ref)
    st = (zero, zero, zero, zero, (jnp.ones((SUBLANES, QB), F32),) * 2)
    first = n_groups % 2
    if first:
        st = step(*st, slot=0)
    st = lax.fori_loop(0, n_groups // 2, lambda _, s: step(*step(*s, slot=first), slot=1 - first), st)
    for h2 in heads:
        values(h2, st[2], st[3], (n_groups - 1) % 2)


def _sb_prompt(qt, kp, vtp, bias):
    bz, _, nq_t, _, _ = qt.shape
    nk = kp.shape[1]
    assert (nq_t * TQ) % QB == 0
    nq = nq_t * TQ // QB
    b2 = bias.astype(F32) * 0.5
    hi = b2.astype(BF16)
    mid = (b2 - hi.astype(F32)).astype(BF16)
    lo = (b2 - hi.astype(F32) - mid.astype(F32)).astype(BF16)
    rows = jnp.stack([hi, mid, lo], axis=1)
    bias_rows = jnp.zeros((SB_HEADS, LANES, QB), BF16).at[:, :3, :].set(
        jnp.broadcast_to(rows[:, :, None], (SB_HEADS, 3, QB)))
    bias_rows = bias_rows.reshape(SB_HEADS // 2, 2, LANES, QB)
    ones_cols = jnp.zeros((GK, LANES), BF16).at[:, :3].set(1.0)
    pairs = SB_HEADS // 2
    return pl.pallas_call(
        functools.partial(_sb_prompt_kernel, nq=nq),
        grid=(bz * pairs,),
        in_specs=[
            pl.BlockSpec((None, 2, nq_t, SB_DH, TQ), lambda g: (g // pairs, g % pairs, 0, 0, 0)),
            pl.BlockSpec((None, nk, TK, LANES), lambda g: (g // pairs, 0, 0, g % pairs)),
            pl.BlockSpec((None, 2, nk, SB_DH, TK), lambda g: (g // pairs, g % pairs, 0, 0, 0)),
            pl.BlockSpec((None, 2, LANES, QB), lambda g: (g % pairs, 0, 0, 0)),
            _const_spec((GK, LANES)),
        ],
        out_specs=pl.BlockSpec((None, 2, nq_t, SB_DH, TQ), lambda g: (g // pairs, g % pairs, 0, 0, 0)),
        out_shape=jax.ShapeDtypeStruct((bz, SB_HEADS, nq_t, SB_DH, TQ), F32),
        scratch_shapes=[pltpu.VMEM((2, 2, GK, QB), F32), pltpu.VMEM((2, 2, GK, QB), BF16)],
        compiler_params=pltpu.CompilerParams(dimension_semantics=("arbitrary",),
                                             vmem_limit_bytes=VMEM_LIMIT),
        name="sb_prompt",
    )(qt, kp, vtp, bias_rows, ones_cols)


def _sb_out_t_kernel(ot_ref, gt_ref, x_ref, w_ref, y_ref):
    o_t = ot_ref[...].reshape(SB_W, TQ)
    gated = (o_t * _silu(gt_ref[...])).T.astype(BF16)
    y_ref[0] = x_ref[0] + _dot(gated, w_ref[...])


def _sb_out_t(ot, gt, x, w_out):
    bz, _, nq, _, _ = ot.shape
    x_spec = pl.BlockSpec((1, TQ, D_MODEL), lambda b, l: (b, l, 0))
    return pl.pallas_call(
        _sb_out_t_kernel,
        grid=(bz, nq),
        in_specs=[
            pl.BlockSpec((None, SB_HEADS, None, SB_DH, TQ), lambda b, l: (b, 0, l, 0, 0)),
            pl.BlockSpec((None, None, SB_W, TQ), lambda b, l: (b, l, 0, 0)),
            x_spec,
            _const_spec((SB_W, D_MODEL)),
        ],
        out_specs=x_spec,
        out_shape=jax.ShapeDtypeStruct(x.shape, F32),
        compiler_params=pltpu.CompilerParams(dimension_semantics=("arbitrary", "arbitrary"),
                                             vmem_limit_bytes=VMEM_LIMIT),
        name="sb_out_prompt",
    )(ot, gt, x, w_out.astype(BF16))


PAGES_PER_STEP = 8


def _rev_cumprod_rows(m, carry):
    rows, n = m.shape
    rid = lax.broadcasted_iota(jnp.int32, (rows, n), 0) % SUBLANES
    p = m
    for s in (1, 2, 4):
        p = p * jnp.where(rid < SUBLANES - s, pltpu.roll(p, rows - s, axis=0), 1.0)
    nv = rows // SUBLANES
    p3 = p.reshape(nv, SUBLANES, n)
    out = [None] * nv
    for i in range(nv - 1, -1, -1):
        out[i] = p3[i] * carry
        carry = jnp.broadcast_to(out[i][0:1], (SUBLANES, n))
    return jnp.concatenate(out, axis=0), carry


def _sb_sample_kernel(pt_ref, q_ref, kn_ref, vn_ref, bias_ref, *refs, seq, n_steps):
    kpages = refs[:PAGES_PER_STEP]
    vpages = refs[PAGES_PER_STEP:2 * PAGES_PER_STEP]
    o_ref = refs[2 * PAGES_PER_STEP]
    qbd_scr, carry_scr, acc_scr, kt16_scr = refs[2 * PAGES_PER_STEP + 1:]
    step = pl.program_id(1)
    hq = SB_HEADS * seq

    def attend(u, vt16, carry, mask=None):
        w = jnp.exp2(jnp.minimum(u + bias_ref[...], Z2_MAX))
        if mask is not None:
            w = jnp.where(mask, w, 0.0)
        prod, carry = _rev_cumprod_rows(1.0 + w, carry)
        return _dot(vt16, (w / prod).astype(BF16)), carry

    @pl.when(step == 0)
    def _():
        q = q_ref[...]
        qrow = jnp.concatenate([q] * SB_HEADS, axis=0)
        rh = lax.broadcasted_iota(jnp.int32, (hq, SB_W), 0) // seq
        lh = lax.broadcasted_iota(jnp.int32, (hq, SB_W), 1) // SB_DH
        qbd = jnp.where(rh == lh, qrow, 0.0).astype(BF16)
        qbd_scr[...] = qbd
        pad = jnp.zeros((PAGE - seq, SB_W), F32)
        k_new = jnp.concatenate([kn_ref[...], pad], axis=0).astype(BF16)
        vt_new = jnp.concatenate([vn_ref[...], pad], axis=0).T.astype(BF16)
        lane = lax.broadcasted_iota(jnp.int32, (PAGE, hq), 1)
        krow = lax.broadcasted_iota(jnp.int32, (PAGE, hq), 0)
        o, carry = attend(_dot_nt(k_new, qbd), vt_new, jnp.ones((SUBLANES, hq), F32), krow < (lane % seq))
        acc_scr[...] = o
        carry_scr[...] = carry

    for i in range(PAGES_PER_STEP):
        kt16_scr[:, i * PAGE:(i + 1) * PAGE] = kpages[i][...].astype(BF16)
    z_all = _dot(qbd_scr[...], kt16_scr[...])
    carry = carry_scr[...]
    acc = acc_scr[...]
    for i in range(PAGES_PER_STEP):
        o, carry = attend(z_all[:, i * PAGE:(i + 1) * PAGE].T, vpages[i][...].astype(BF16), carry)
        acc = acc + o
    acc_scr[...] = acc
    carry_scr[...] = carry

    @pl.when(step == n_steps - 1)
    def _():
        o3 = acc.T.reshape(SB_HEADS, seq, SB_W)
        hsel = lax.broadcasted_iota(jnp.int32, (SB_HEADS, seq, SB_W), 0)
        lsel = lax.broadcasted_iota(jnp.int32, (SB_HEADS, seq, SB_W), 2) // SB_DH
        o_ref[...] = jnp.sum(jnp.where(hsel == lsel, o3, 0.0), axis=0)


def _sb_sample(q, k_new, v_new, cache_kt, cache_vt, page_table, bias):
    bz, seq, _ = q.shape
    n_pages = page_table.shape[1]
    assert n_pages % PAGES_PER_STEP == 0 and seq == SUBLANES
    n_steps = n_pages // PAGES_PER_STEP
    bias_lane = jnp.repeat(bias.astype(F32) * LOG2E, seq).reshape(1, SB_HEADS * seq)
    tok_spec = pl.BlockSpec((None, seq, SB_W), lambda b, s, pt: (b, 0, 0))

    def page_spec(i):
        def index(b, s, pt):
            return (pt[b * n_pages + n_pages - 1 - (s * PAGES_PER_STEP + i)], 0, 0)
        return pl.BlockSpec((None, SB_W, PAGE), index)

    pages = [page_spec(i) for i in range(PAGES_PER_STEP)]
    hq = SB_HEADS * seq
    return pl.pallas_call(
        functools.partial(_sb_sample_kernel, seq=seq, n_steps=n_steps),
        grid_spec=pltpu.PrefetchScalarGridSpec(
            num_scalar_prefetch=1,
            grid=(bz, n_steps),
            in_specs=[tok_spec, tok_spec, tok_spec, pl.BlockSpec((1, hq), lambda b, s, pt: (0, 0))] + pages + pages,
            out_specs=tok_spec,
            scratch_shapes=[pltpu.VMEM((hq, SB_W), BF16), pltpu.VMEM((SUBLANES, hq), F32),
                            pltpu.VMEM((SB_W, hq), F32), pltpu.VMEM((SB_W, PAGES_PER_STEP * PAGE), BF16)],
        ),
        out_shape=jax.ShapeDtypeStruct((bz, seq, SB_W), F32),
        compiler_params=pltpu.CompilerParams(dimension_semantics=("arbitrary", "arbitrary"),
                                             vmem_limit_bytes=VMEM_LIMIT),
        name="sb_sample",
    )(page_table.reshape(-1), q, k_new, v_new, bias_lane,
      *([cache_kt] * PAGES_PER_STEP), *([cache_vt] * PAGES_PER_STEP))


def _sb_out_kernel(o_ref, g_ref, x_ref, w_ref, y_ref):
    gated = (o_ref[...] * _silu(g_ref[...])).astype(BF16)
    y_ref[...] = x_ref[...] + _dot(gated, w_ref[...])


def _sb_out(o, g, x, w_out, *, tm):
    n = o.shape[0]
    spec = pl.BlockSpec((tm, D_MODEL), lambda i: (i, 0))
    return pl.pallas_call(
        _sb_out_kernel,
        grid=(n // tm,),
        in_specs=[spec, spec, spec, _const_spec((SB_W, D_MODEL))],
        out_specs=spec,
        out_shape=jax.ShapeDtypeStruct((n, D_MODEL), F32),
        compiler_params=pltpu.CompilerParams(dimension_semantics=("arbitrary",), vmem_limit_bytes=VMEM_LIMIT),
        name="sb_out_sample",
    )(o, g, x, w_out.astype(BF16))


def kernel(x_prompt, x_sample, state_gla, cache_k, cache_v, page_table, norm_gain, w_in_a, w_alpha_up, b_alpha, onorm_a, w_out_a, w_in_b, qnorm_b, knorm_b, sb_bias, w_out_b):
    bz, seq_p, _ = x_prompt.shape
    dec_b, dec_s, _ = x_sample.shape

    gla = functools.partial(_gla_layer, gain=norm_gain[0], w_in=w_in_a[0], w_up=w_alpha_up[0], b_a=b_alpha[0],
                            onorm=onorm_a[0], w_out=w_out_a[0])
    xp, st_p = gla(x_prompt, None, nb=1, tm=4 * GLA_CHUNK, chunk=GLA_CHUNK)
    xs, st_s = gla(x_sample, state_gla[0], nb=GLA_GROUP // dec_s, tm=dec_s, chunk=dec_s)

    k_p, v_p, qt, kp, vtp, gt = _sb_in(xp, norm_gain[1], w_in_b[0], qnorm_b[0], knorm_b[0], tm=TQ, prompt=True)
    ot = _sb_prompt(qt, kp, vtp, sb_bias[0])
    yp = _sb_out_t(ot, gt, xp, w_out_b[0])

    xs_flat = xs.reshape(1, dec_b * dec_s, D_MODEL)
    k_s, v_s, q_s, g_s = _sb_in(xs_flat, norm_gain[1], w_in_b[0], qnorm_b[0], knorm_b[0], tm=TQ, prompt=False)
    per_seq = (dec_b, dec_s, SB_W)
    n_phys = cache_k.shape[1]
    cache_kt = jnp.transpose(cache_k[0], (0, 2, 3, 1)).reshape(n_phys, SB_W, PAGE)
    cache_vt = jnp.transpose(cache_v[0], (0, 2, 3, 1)).reshape(n_phys, SB_W, PAGE)
    o_s = _sb_sample(q_s.reshape(per_seq), k_s.reshape(per_seq), v_s.reshape(per_seq),
                     cache_kt, cache_vt, page_table, sb_bias[0])
    ys = _sb_out(o_s.reshape(dec_b * dec_s, SB_W), g_s[0], xs_flat[0], w_out_b[0], tm=TQ)

    heads = (SB_HEADS, SB_DH)

    def from_feature_major(t):
        return jnp.transpose(t.reshape(bz, *heads, seq_p), (0, 3, 1, 2))[None]

    return (yp, ys.reshape(dec_b, dec_s, D_MODEL), st_p[None], st_s[None],
            from_feature_major(k_p), from_feature_major(v_p),
            k_s.reshape(1, dec_b, dec_s, *heads), v_s.reshape(1, dec_b, dec_s, *heads))
```

```python
import functools
import math

import jax
import jax.numpy as jnp
import numpy as np
from jax import lax
from jax.experimental import pallas as pl
from jax.experimental.pallas import tpu as pltpu

EPS = 1e-6
LOG2E = math.log2(math.e)

D_MODEL = 1024
GLA_HEADS = 4
GLA_DK = 128
GLA_DV = 256
GLA_KEY_W = GLA_HEADS * GLA_DK
GLA_VAL_W = GLA_HEADS * GLA_DV
GLA_RANK = 16
GLA_TAU = 16.0
GLA_CHUNK = 64
GLA_GROUP = 64
SB_HEADS = 16
SB_DH = 64
SB_W = SB_HEADS * SB_DH
PAGE = 128
LANES = 128
SUBLANES = 8
VMEM_LIMIT = 56 * 1024 * 1024

TQ = 256
TK = 128
QB = 1024
GK = 1024
Z2_MAX = 80.0

BF16 = jnp.bfloat16
F32 = jnp.float32


def _dot(a, b):
    return jnp.dot(a, b, preferred_element_type=F32)


def _dot_nt(a, b):
    return lax.dot_general(a, b, (((1,), (1,)), ((), ())), preferred_element_type=F32)


def _split2(x):
    hi = x.astype(BF16)
    lo = (x - hi.astype(F32)).astype(BF16)
    return hi, lo


def _log_sigmoid(x):
    return jnp.minimum(x, 0.0) - jnp.log(1.0 + jnp.exp(-jnp.abs(x)))


def _silu(x):
    return x / (1.0 + jnp.exp(-x))


def _rms_rows(x, gain):
    ms = jnp.mean(x * x, axis=-1, keepdims=True)
    return x * lax.rsqrt(ms + EPS) * gain


def _const_spec(shape):
    zeros = (0,) * len(shape)
    return pl.BlockSpec(shape, lambda *_: zeros, pipeline_mode=pl.Buffered(1))


def _gla_kernel(*refs, nb, tm, chunk, has_s0):
    if has_s0:
        (x_ref, s0_ref, gain_ref, wm_ref, wkt_ref, wac_ref, wact_ref, wup_ref, wupt_ref, ba_ref, bat_ref,
         on_ref, wout_ref, lcum_ref, mlater_ref, y_ref, sout_ref, s_scr, o_scr) = refs
    else:
        (x_ref, gain_ref, wm_ref, wkt_ref, wac_ref, wact_ref, wup_ref, wupt_ref, ba_ref, bat_ref,
         on_ref, wout_ref, lcum_ref, mlater_ref, y_ref, sout_ref, s_scr, o_scr) = refs
        s0_ref = None
    m = nb * tm
    step = pl.program_id(1)

    @pl.when(step == 0)
    def _():
        if has_s0:
            s_scr[...] = s0_ref[...]
        else:
            s_scr[...] = jnp.zeros_like(s_scr)

    x = x_ref[...].reshape(m, D_MODEL)
    h = _rms_rows(x, gain_ref[...]).astype(BF16)
    u = _dot(h, wm_ref[...])
    q = u[:, :GLA_KEY_W] * (GLA_DK ** -0.5)
    k = u[:, GLA_KEY_W:2 * GLA_KEY_W]
    v16 = u[:, 2 * GLA_KEY_W:2 * GLA_KEY_W + GLA_VAL_W].astype(BF16)
    gate = u[:, 2 * GLA_KEY_W + GLA_VAL_W:]

    code = _dot(h, wac_ref[...]).astype(BF16)
    la = _log_sigmoid(_dot(code, wup_ref[...]) + ba_ref[...]) * (1.0 / GLA_TAU)
    code_t = _dot_nt(wact_ref[...], h).astype(BF16)
    la_t = _log_sigmoid(_dot(wupt_ref[...], code_t) + bat_ref[...]) * (1.0 / GLA_TAU)
    k_t = _dot_nt(wkt_ref[...], h)

    la_hi, la_lo = _split2(la)
    b = _dot(lcum_ref[...], la_hi) + _dot(lcum_ref[...], la_lo)
    qd = (q * jnp.exp(b)).astype(BF16)
    kd = (k * jnp.exp(-b)).astype(BF16)
    lat_hi, lat_lo = _split2(la_t)
    rest_t = _dot(lat_hi, mlater_ref[...]) + _dot(lat_lo, mlater_ref[...])
    kl_t = k_t * jnp.exp(rest_t)

    lb = min(LANES, m)
    n_groups = m // GLA_GROUP
    n_sub = GLA_GROUP // chunk
    ri = lax.broadcasted_iota(jnp.int32, (GLA_GROUP, GLA_GROUP), 0)
    ci = lax.broadcasted_iota(jnp.int32, (GLA_GROUP, GLA_GROUP), 1)
    att_mask = ((ri // chunk) == (ci // chunk)) & (ci <= ri)

    for g in range(n_groups):
        r0 = g * GLA_GROUP
        rows = slice(r0, r0 + GLA_GROUP)
        blk = (r0 // lb) * lb
        cols = slice(blk, blk + lb)
        for hd in range(GLA_HEADS):
            fk = slice(hd * GLA_DK, (hd + 1) * GLA_DK)
            fv = slice(hd * GLA_DV, (hd + 1) * GLA_DV)
            qd_g = qd[rows, fk]
            att = jnp.where(att_mask, _dot_nt(qd_g, kd[rows, fk]), 0.0).astype(BF16)
            o = _dot(att, v16[rows, fv])
            for j in range(n_sub):
                sub = (r0 + j * chunk) // chunk
                si = (r0 + j * chunk) // tm
                s_old = s_scr[si, hd]
                o_st = _dot(qd_g, s_old.astype(BF16))
                if n_sub > 1:
                    rsel = lax.broadcasted_iota(jnp.int32, (GLA_GROUP, GLA_DV), 0)
                    o_st = jnp.where((rsel + r0) // chunk == sub, o_st, 0.0)
                o = o + o_st
                csel = (lax.broadcasted_iota(jnp.int32, (GLA_DK, lb), 1) + blk) // chunk == sub
                la_sel = jnp.where(csel, la_t[fk, cols], 0.0)
                kl_sel = jnp.where(csel, kl_t[fk, cols], 0.0).astype(BF16)
                decay = jnp.exp(jnp.sum(la_sel, axis=1, keepdims=True))
                s_scr[si, hd] = s_old * decay + _dot(kl_sel, v16[cols, fv])
            o_scr[rows, fv] = o

    y = x
    for hd in range(GLA_HEADS):
        fv = slice(hd * GLA_DV, (hd + 1) * GLA_DV)
        on = _rms_rows(o_scr[:, fv], on_ref[...])
        gated = (on * _silu(gate[:, fv])).astype(BF16)
        y = y + _dot(gated, wout_ref[fv, :])
    y_ref[...] = y.reshape(nb, tm, D_MODEL)

    @pl.when(step == pl.num_programs(1) - 1)
    def _():
        sout_ref[...] = s_scr[...]


def _block_structure(m, chunk):
    idx = np.arange(m)
    same = (idx[:, None] // chunk) == (idx[None, :] // chunk)
    lcum = same & (idx[None, :] <= idx[:, None])
    later = same & (idx[:, None] > idx[None, :])
    return jnp.asarray(lcum, BF16), jnp.asarray(later, BF16)


def _gla_layer(x, s0, gain, w_in, w_up, b_a, onorm, w_out, *, nb, tm, chunk):
    bz, length, _ = x.shape
    has_s0 = s0 is not None
    m = nb * tm
    assert bz % nb == 0 and length % tm == 0 and m % GLA_GROUP == 0
    assert chunk == GLA_GROUP or (chunk == tm and m == GLA_GROUP)
    wm = w_in[:, :2 * GLA_KEY_W + 2 * GLA_VAL_W].astype(BF16)
    wkt = w_in[:, GLA_KEY_W:2 * GLA_KEY_W].T.astype(BF16)
    wac = jnp.pad(w_in[:, 2 * GLA_KEY_W + 2 * GLA_VAL_W:], ((0, 0), (0, LANES - GLA_RANK))).astype(BF16)
    wup = jnp.pad(w_up, ((0, LANES - GLA_RANK), (0, 0))).astype(BF16)
    lcum, later = _block_structure(m, chunk)
    consts = [gain.reshape(1, D_MODEL), wm, wkt, wac, wac.T, wup, wup.T, b_a.reshape(1, GLA_KEY_W),
              b_a.reshape(GLA_KEY_W, 1), onorm.reshape(1, GLA_DV), w_out.astype(BF16), lcum, later]
    state_shape = (nb, GLA_HEADS, GLA_DK, GLA_DV)
    state_spec = pl.BlockSpec(state_shape, lambda b, l: (b, 0, 0, 0))
    x_spec = pl.BlockSpec((nb, tm, D_MODEL), lambda b, l: (b, l, 0))
    in_specs = [x_spec] + ([state_spec] if has_s0 else []) + [_const_spec(c.shape) for c in consts]
    args = [x] + ([s0] if has_s0 else []) + consts
    return pl.pallas_call(
        functools.partial(_gla_kernel, nb=nb, tm=tm, chunk=chunk, has_s0=has_s0),
        grid=(bz // nb, length // tm),
        in_specs=in_specs,
        out_specs=[x_spec, state_spec],
        out_shape=[jax.ShapeDtypeStruct(x.shape, F32),
                   jax.ShapeDtypeStruct((bz, GLA_HEADS, GLA_DK, GLA_DV), F32)],
        scratch_shapes=[pltpu.VMEM(state_shape, F32), pltpu.VMEM((m, GLA_VAL_W), F32)],
        compiler_params=pltpu.CompilerParams(dimension_semantics=("arbitrary", "arbitrary"),
                                             vmem_limit_bytes=VMEM_LIMIT),
        name="gla_layer_s0" if has_s0 else "gla_layer",
    )(*args)


def _key_permutation():
    p = np.arange(TK)
    return (TK // SUBLANES) * (p % SUBLANES) + p // SUBLANES


def _head_norm_t(t, gain_col, tokens):
    t3 = t.reshape(SB_HEADS, SB_DH, tokens)
    ms = jnp.mean(t3 * t3, axis=1, keepdims=True)
    return (t3 * lax.rsqrt(ms + EPS)).reshape(SB_W, tokens) * gain_col


def _sb_in_kernel(*refs, tm, prompt):
    if prompt:
        (x_ref, gain_ref, wt_ref, qg_ref, kg_ref, p_ref, pt_ref,
         k_ref, v_ref, qt_ref, kp_ref, vtp_ref, gt_ref) = refs
    else:
        (x_ref, gain_ref, wt_ref, qg_ref, kg_ref, k_ref, v_ref, q_ref, g_ref) = refs
    x = x_ref[0]
    h = _rms_rows(x, gain_ref[...]).astype(BF16)
    u_t = _dot_nt(wt_ref[...], h)
    q_t = _head_norm_t(u_t[:SB_W], qg_ref[...], tm) * (SB_DH ** -0.5 * (0.5 if prompt else LOG2E))
    k_t = _head_norm_t(u_t[SB_W:2 * SB_W], kg_ref[...], tm)
    v_t = u_t[2 * SB_W:3 * SB_W]
    g_t = u_t[3 * SB_W:]
    if prompt:
        k_ref[0] = k_t
        v_ref[0] = v_t
        qt_ref[...] = q_t.astype(BF16).reshape(SB_HEADS, SB_DH, tm)
        gt_ref[...] = g_t
        kp_ref[...] = _dot_nt(p_ref[...], k_t.astype(BF16)).astype(BF16).reshape(tm // TK, TK, SB_W)
        vp_t = _dot(v_t.astype(BF16), pt_ref[...])
        for j in range(tm // TK):
            vtp_ref[:, j] = vp_t[:, j * TK:(j + 1) * TK].astype(BF16).reshape(SB_HEADS, SB_DH, TK)
    else:
        k_ref[0] = k_t.T
        v_ref[0] = v_t.T
        q_ref[0] = q_t.T
        g_ref[0] = g_t.T


def _sb_in(x, gain, w_in, qn, kn, *, tm, prompt):
    bz, length, _ = x.shape
    assert length % tm == 0 and tm % TK == 0
    nt = length // tm
    wt = w_in.T.astype(BF16)
    qg = jnp.tile(qn, SB_HEADS).reshape(SB_W, 1)
    kg = jnp.tile(kn, SB_HEADS).reshape(SB_W, 1)
    consts = [gain.reshape(1, D_MODEL), wt, qg, kg]
    nat_spec = pl.BlockSpec((1, tm, SB_W), lambda b, l: (b, l, 0))
    nat_shape = jax.ShapeDtypeStruct((bz, length, SB_W), F32)
    if prompt:
        assert tm == TQ
        perm = np.concatenate([_key_permutation() + j * TK for j in range(tm // TK)])
        pt = np.zeros((tm, tm), np.float32)
        pt[perm, np.arange(tm)] = 1.0
        consts += [jnp.asarray(pt.T, BF16), jnp.asarray(pt, BF16)]
        t_spec = pl.BlockSpec((1, SB_W, tm), lambda b, l: (b, 0, l))
        t_shape = jax.ShapeDtypeStruct((bz, SB_W, length), F32)
        out_specs = [
            t_spec, t_spec,
            pl.BlockSpec((None, SB_HEADS, None, SB_DH, tm), lambda b, l: (b, 0, l, 0, 0)),
            pl.BlockSpec((None, tm // TK, TK, SB_W), lambda b, l: (b, l, 0, 0)),
            pl.BlockSpec((None, SB_HEADS, tm // TK, SB_DH, TK), lambda b, l: (b, 0, l, 0, 0)),
            pl.BlockSpec((None, None, SB_W, tm), lambda b, l: (b, l, 0, 0)),
        ]
        out_shape = [
            t_shape, t_shape,
            jax.ShapeDtypeStruct((bz, SB_HEADS, nt, SB_DH, tm), BF16),
            jax.ShapeDtypeStruct((bz, length // TK, TK, SB_W), BF16),
            jax.ShapeDtypeStruct((bz, SB_HEADS, length // TK, SB_DH, TK), BF16),
            jax.ShapeDtypeStruct((bz, nt, SB_W, tm), F32),
        ]
    else:
        out_specs = [nat_spec] * 4
        out_shape = [nat_shape] * 4
    return pl.pallas_call(
        functools.partial(_sb_in_kernel, tm=tm, prompt=prompt),
        grid=(bz, nt),
        in_specs=[nat_spec] + [_const_spec(c.shape) for c in consts],
        out_specs=out_specs,
        out_shape=out_shape,
        compiler_params=pltpu.CompilerParams(dimension_semantics=("arbitrary", "arbitrary"),
                                             vmem_limit_bytes=VMEM_LIMIT),
        name="sb_in_prompt" if prompt else "sb_in_sample",
    )(x, *consts)


def _shift_rows_up(x, s, rid):
    return jnp.where(rid < SUBLANES - s, pltpu.roll(x, SUBLANES - s, axis=0), 1.0)


def _sb_tile(u, carry, rid, mask=None):
    n = u.shape[-1]
    rho = 0.5 - 0.5 * jnp.tanh(u)
    if mask is not None:
        rho = jnp.where(mask, rho, 1.0)
    nv = TK // SUBLANES
    r3 = rho.reshape(nv, SUBLANES, n)
    run = [None] * nv
    run[nv - 1] = r3[nv - 1]
    for i in range(nv - 2, -1, -1):
        run[i] = r3[i] * run[i + 1]
    tot = run[0]
    later = _shift_rows_up(tot, 1, rid)
    later = later * _shift_rows_up(later, 1, rid)
    later = later * _shift_rows_up(later, 2, rid)
    later = later * _shift_rows_up(later, 4, rid)
    grp = later * carry
    run.append(jnp.ones_like(tot))
    diff = jnp.concatenate([run[i + 1] - run[i] for i in range(nv)], axis=0).astype(BF16)
    grp2 = jnp.concatenate([grp, grp], axis=0).astype(BF16)
    a = (diff.reshape(nv // 2, 2 * SUBLANES, n) * grp2[None]).reshape(TK, n)
    new_carry = jnp.broadcast_to((grp * tot)[0:1], (SUBLANES, n))
    return a, new_carry


def _sb_prompt_kernel(qt_ref, kp_ref, vtp_ref, bias_ref, ones_ref, ot_ref, u_scr, a_scr, *, nq):
    rid = lax.broadcasted_iota(jnp.int32, (SUBLANES, LANES), 0)
    prow = lax.broadcasted_iota(jnp.int32, (TK, LANES), 0)
    key_in_tile = (TK // SUBLANES) * (prow % SUBLANES) + prow // SUBLANES
    qlane = lax.broadcasted_iota(jnp.int32, (TK, LANES), 1)
    zeros_q = jnp.zeros((SB_DH, QB), BF16)
    tiles = GK // TK
    halves = QB // TQ
    n_groups = nq * (nq + 1) // 2

    heads = range(2)

    def scores(h2, qb, g, slot):
        q = jnp.concatenate([qt_ref[h2, halves * qb + i] for i in range(halves)], axis=1)
        parts = [q, zeros_q] if h2 == 0 else [zeros_q, q]
        qaug = jnp.concatenate(parts + [bias_ref[h2]], axis=0)
        keys = kp_ref[pl.ds(g * tiles, tiles)].reshape(GK, LANES)
        u_scr[slot, h2] = _dot(jnp.concatenate([keys, ones_ref[...]], axis=1), qaug)

    def weights(h2, slot, carry, diag):
        out = []
        for c in range(QB // LANES):
            cs = slice(c * LANES, (c + 1) * LANES)
            car = carry[:, cs]
            for t in range(tiles - 1, -1, -1):
                rows = slice(t * TK, (t + 1) * TK)
                lead = t * TK - c * LANES
                if diag and lead >= LANES:
                    a_scr[slot, h2, rows, cs] = jnp.zeros((TK, LANES), BF16)
                    continue
                mask = (key_in_tile + lead) < qlane if diag and lead > -TK else None
                a, car = _sb_tile(u_scr[slot, h2, rows, cs], car, rid, mask)
                a_scr[slot, h2, rows, cs] = a
            out.append(car)
        return jnp.concatenate(out, axis=1)

    def values(h2, qb, g, slot):
        vt = jnp.concatenate([vtp_ref[h2, g * tiles + t] for t in range(tiles)], axis=1)
        part = _dot(vt, a_scr[slot, h2])
        for i in range(halves):
            prev = jnp.where(g == qb, 0.0, ot_ref[h2, halves * qb + i])
            ot_ref[h2, halves * qb + i] = prev + part[:, i * TQ:(i + 1) * TQ]

    def step(qb, g, pqb, pg, carries, *, slot):
        last = g == 0
        nqb = jnp.minimum(jnp.where(last, qb + 1, qb), nq - 1)
        ng = jnp.where(last, nqb, g - 1)

        def body(diag):
            def run(carries):
                ones = jnp.ones((SUBLANES, QB), F32)
                carries = tuple(weights(h2, slot, ones if diag else carries[h2], diag) for h2 in heads)
                for h2 in heads:
                    values(h2, pqb, pg, 1 - slot)
                    scores(h2, nqb, ng, 1 - slot)
                return carries
            return run

        return nqb, ng, qb, g, lax.cond(g == qb, body(True), body(False), carries)

    zero = jnp.zeros((), jnp.int32)
    for h2 in heads:
        scores(h2, zero, zero, 0)
    a_scr[1] = jnp.zeros((2, GK, QB), BF16)
    ot_ref[...] = jnp.zeros_like(ot_ref)
    st = (zero, zero, zero, zero, (jnp.ones((SUBLANES, QB), F32),) * 2)
    first = n_groups % 2
    if first:
        st = step(*st, slot=0)
    st = lax.fori_loop(0, n_groups // 2, lambda _, s: step(*step(*s, slot=first), slot=1 - first), st)
    for h2 in heads:
        values(h2, st[2], st[3], (n_groups - 1) % 2)


def _sb_prompt(qt, kp, vtp, bias):
    bz, _, nq_t, _, _ = qt.shape
    nk = kp.shape[1]
    assert (nq_t * TQ) % QB == 0
    nq = nq_t * TQ // QB
    b2 = bias.astype(F32) * 0.5
    hi = b2.astype(BF16)
    mid = (b2 - hi.astype(F32)).astype(BF16)
    lo = (b2 - hi.astype(F32) - mid.astype(F32)).astype(BF16)
    rows = jnp.stack([hi, mid, lo], axis=1)
    bias_rows = jnp.zeros((SB_HEADS, LANES, QB), BF16).at[:, :3, :].set(
        jnp.broadcast_to(rows[:, :, None], (SB_HEADS, 3, QB)))
    bias_rows = bias_rows.reshape(SB_HEADS // 2, 2, LANES, QB)
    ones_cols = jnp.zeros((GK, LANES), BF16).at[:, :3].set(1.0)
    pairs = SB_HEADS // 2
    return pl.pallas_call(
        functools.partial(_sb_prompt_kernel, nq=nq),
        grid=(bz * pairs,),
        in_specs=[
            pl.BlockSpec((None, 2, nq_t, SB_DH, TQ), lambda g: (g // pairs, g % pairs, 0, 0, 0)),
            pl.BlockSpec((None, nk, TK, LANES), lambda g: (g // pairs, 0, 0, g % pairs)),
            pl.BlockSpec((None, 2, nk, SB_DH, TK), lambda g: (g // pairs, g % pairs, 0, 0, 0)),
            pl.BlockSpec((None, 2, LANES, QB), lambda g: (g % pairs, 0, 0, 0)),
            _const_spec((GK, LANES)),
        ],
        out_specs=pl.BlockSpec((None, 2, nq_t, SB_DH, TQ), lambda g: (g // pairs, g % pairs, 0, 0, 0)),
        out_shape=jax.ShapeDtypeStruct((bz, SB_HEADS, nq_t, SB_DH, TQ), F32),
        scratch_shapes=[pltpu.VMEM((2, 2, GK, QB), F32), pltpu.VMEM((2, 2, GK, QB), BF16)],
        compiler_params=pltpu.CompilerParams(dimension_semantics=("arbitrary",),
                                             vmem_limit_bytes=VMEM_LIMIT),
        name="sb_prompt",
    )(qt, kp, vtp, bias_rows, ones_cols)


def _sb_out_t_kernel(ot_ref, gt_ref, x_ref, w_ref, y_ref):
    o_t = ot_ref[...].reshape(SB_W, TQ)
    gated = (o_t * _silu(gt_ref[...])).T.astype(BF16)
    y_ref[0] = x_ref[0] + _dot(gated, w_ref[...])


def _sb_out_t(ot, gt, x, w_out):
    bz, _, nq, _, _ = ot.shape
    x_spec = pl.BlockSpec((1, TQ, D_MODEL), lambda b, l: (b, l, 0))
    return pl.pallas_call(
        _sb_out_t_kernel,
        grid=(bz, nq),
        in_specs=[
            pl.BlockSpec((None, SB_HEADS, None, SB_DH, TQ), lambda b, l: (b, 0, l, 0, 0)),
            pl.BlockSpec((None, None, SB_W, TQ), lambda b, l: (b, l, 0, 0)),
            x_spec,
            _const_spec((SB_W, D_MODEL)),
        ],
        out_specs=x_spec,
        out_shape=jax.ShapeDtypeStruct(x.shape, F32),
        compiler_params=pltpu.CompilerParams(dimension_semantics=("arbitrary", "arbitrary"),
                                             vmem_limit_bytes=VMEM_LIMIT),
        name="sb_out_prompt",
    )(ot, gt, x, w_out.astype(BF16))


PAGES_PER_STEP = 8


def _rev_cumprod_rows(m, carry):
    rows, n = m.shape
    rid = lax.broadcasted_iota(jnp.int32, (rows, n), 0) % SUBLANES
    p = m
    for s in (1, 2, 4):
        p = p * jnp.where(rid < SUBLANES - s, pltpu.roll(p, rows - s, axis=0), 1.0)
    nv = rows // SUBLANES
    p3 = p.reshape(nv, SUBLANES, n)
    out = [None] * nv
    for i in range(nv - 1, -1, -1):
        out[i] = p3[i] * carry
        carry = jnp.broadcast_to(out[i][0:1], (SUBLANES, n))
    return jnp.concatenate(out, axis=0), carry


def _sb_sample_kernel(pt_ref, q_ref, kn_ref, vn_ref, bias_ref, *refs, seq, n_steps):
    kpages = refs[:PAGES_PER_STEP]
    vpages = refs[PAGES_PER_STEP:2 * PAGES_PER_STEP]
    o_ref = refs[2 * PAGES_PER_STEP]
    qbd_scr, carry_scr, acc_scr, kt16_scr = refs[2 * PAGES_PER_STEP + 1:]
    step = pl.program_id(1)
    hq = SB_HEADS * seq

    def attend(u, vt16, carry, mask=None):
        w = jnp.exp2(jnp.minimum(u + bias_ref[...], Z2_MAX))
        if mask is not None:
            w = jnp.where(mask, w, 0.0)
        prod, carry = _rev_cumprod_rows(1.0 + w, carry)
        return _dot(vt16, (w / prod).astype(BF16)), carry

    @pl.when(step == 0)
    def _():
        q = q_ref[...]
        qrow = jnp.concatenate([q] * SB_HEADS, axis=0)
        rh = lax.broadcasted_iota(jnp.int32, (hq, SB_W), 0) // seq
        lh = lax.broadcasted_iota(jnp.int32, (hq, SB_W), 1) // SB_DH
        qbd = jnp.where(rh == lh, qrow, 0.0).astype(BF16)
        qbd_scr[...] = qbd
        pad = jnp.zeros((PAGE - seq, SB_W), F32)
        k_new = jnp.concatenate([kn_ref[...], pad], axis=0).astype(BF16)
        vt_new = jnp.concatenate([vn_ref[...], pad], axis=0).T.astype(BF16)
        lane = lax.broadcasted_iota(jnp.int32, (PAGE, hq), 1)
        krow = lax.broadcasted_iota(jnp.int32, (PAGE, hq), 0)
        o, carry = attend(_dot_nt(k_new, qbd), vt_new, jnp.ones((SUBLANES, hq), F32), krow < (lane % seq))
        acc_scr[...] = o
        carry_scr[...] = carry

    for i in range(PAGES_PER_STEP):
        kt16_scr[:, i * PAGE:(i + 1) * PAGE] = kpages[i][...].astype(BF16)
    z_all = _dot(qbd_scr[...], kt16_scr[...])
    carry = carry_scr[...]
    acc = acc_scr[...]
    for i in range(PAGES_PER_STEP):
        o, carry = attend(z_all[:, i * PAGE:(i + 1) * PAGE].T, vpages[i][...].astype(BF16), carry)
        acc = acc + o
    acc_scr[...] = acc
    carry_scr[...] = carry

    @pl.when(step == n_steps - 1)
    def _():
        o3 = acc.T.reshape(SB_HEADS, seq, SB_W)
        hsel = lax.broadcasted_iota(jnp.int32, (SB_HEADS, seq, SB_W), 0)
        lsel = lax.broadcasted_iota(jnp.int32, (SB_HEADS, seq, SB_W), 2) // SB_DH
        o_ref[...] = jnp.sum(jnp.where(hsel == lsel, o3, 0.0), axis=0)


def _sb_sample(q, k_new, v_new, cache_kt, cache_vt, page_table, bias):
    bz, seq, _ = q.shape
    n_pages = page_table.shape[1]
    assert n_pages % PAGES_PER_STEP == 0 and seq == SUBLANES
    n_steps = n_pages // PAGES_PER_STEP
    bias_lane = jnp.repeat(bias.astype(F32) * LOG2E, seq).reshape(1, SB_HEADS * seq)
    tok_spec = pl.BlockSpec((None, seq, SB_W), lambda b, s, pt: (b, 0, 0))

    def page_spec(i):
        def index(b, s, pt):
            return (pt[b * n_pages + n_pages - 1 - (s * PAGES_PER_STEP + i)], 0, 0)
        return pl.BlockSpec((None, SB_W, PAGE), index)

    pages = [page_spec(i) for i in range(PAGES_PER_STEP)]
    hq = SB_HEADS * seq
    return pl.pallas_call(
        functools.partial(_sb_sample_kernel, seq=seq, n_steps=n_steps),
        grid_spec=pltpu.PrefetchScalarGridSpec(
            num_scalar_prefetch=1,
            grid=(bz, n_steps),
            in_specs=[tok_spec, tok_spec, tok_spec, pl.BlockSpec((1, hq), lambda b, s, pt: (0, 0))] + pages + pages,
            out_specs=tok_spec,
            scratch_shapes=[pltpu.VMEM((hq, SB_W), BF16), pltpu.VMEM((SUBLANES, hq), F32),
                            pltpu.VMEM((SB_W, hq), F32), pltpu.VMEM((SB_W, PAGES_PER_STEP * PAGE), BF16)],
        ),
        out_shape=jax.ShapeDtypeStruct((bz, seq, SB_W), F32),
        compiler_params=pltpu.CompilerParams(dimension_semantics=("arbitrary", "arbitrary"),
                                             vmem_limit_bytes=VMEM_LIMIT),
        name="sb_sample",
    )(page_table.reshape(-1), q, k_new, v_new, bias_lane,
      *([cache_kt] * PAGES_PER_STEP), *([cache_vt] * PAGES_PER_STEP))


def _sb_out_kernel(o_ref, g_ref, x_ref, w_ref, y_ref):
    gated = (o_ref[...] * _silu(g_ref[...])).astype(BF16)
    y_ref[...] = x_ref[...] + _dot(gated, w_ref[...])


def _sb_out(o, g, x, w_out, *, tm):
    n = o.shape[0]
    spec = pl.BlockSpec((tm, D_MODEL), lambda i: (i, 0))
    return pl.pallas_call(
        _sb_out_kernel,
        grid=(n // tm,),
        in_specs=[spec, spec, spec, _const_spec((SB_W, D_MODEL))],
        out_specs=spec,
        out_shape=jax.ShapeDtypeStruct((n, D_MODEL), F32),
        compiler_params=pltpu.CompilerParams(dimension_semantics=("arbitrary",), vmem_limit_bytes=VMEM_LIMIT),
        name="sb_out_sample",
    )(o, g, x, w_out.astype(BF16))


def kernel(x_prompt, x_sample, state_gla, cache_k, cache_v, page_table, norm_gain, w_in_a, w_alpha_up, b_alpha, onorm_a, w_out_a, w_in_b, qnorm_b, knorm_b, sb_bias, w_out_b):
    bz, seq_p, _ = x_prompt.shape
    dec_b, dec_s, _ = x_sample.shape

    gla = functools.partial(_gla_layer, gain=norm_gain[0], w_in=w_in_a[0], w_up=w_alpha_up[0], b_a=b_alpha[0],
                            onorm=onorm_a[0], w_out=w_out_a[0])
    xp, st_p = gla(x_prompt, None, nb=1, tm=4 * GLA_CHUNK, chunk=GLA_CHUNK)
    xs, st_s = gla(x_sample, state_gla[0], nb=GLA_GROUP // dec_s, tm=dec_s, chunk=dec_s)

    k_p, v_p, qt, kp, vtp, gt = _sb_in(xp, norm_gain[1], w_in_b[0], qnorm_b[0], knorm_b[0], tm=TQ, prompt=True)
    ot = _sb_prompt(qt, kp, vtp, sb_bias[0])
    yp = _sb_out_t(ot, gt, xp, w_out_b[0])

    xs_flat = xs.reshape(1, dec_b * dec_s, D_MODEL)
    k_s, v_s, q_s, g_s = _sb_in(xs_flat, norm_gain[1], w_in_b[0], qnorm_b[0], knorm_b[0], tm=TQ, prompt=False)
    per_seq = (dec_b, dec_s, SB_W)
    n_phys = cache_k.shape[1]
    cache_kt = jnp.transpose(cache_k[0], (0, 2, 3, 1)).reshape(n_phys, SB_W, PAGE)
    cache_vt = jnp.transpose(cache_v[0], (0, 2, 3, 1)).reshape(n_phys, SB_W, PAGE)
    o_s = _sb_sample(q_s.reshape(per_seq), k_s.reshape(per_seq), v_s.reshape(per_seq),
                     cache_kt, cache_vt, page_table, sb_bias[0])
    ys = _sb_out(o_s.reshape(dec_b * dec_s, SB_W), g_s[0], xs_flat[0], w_out_b[0], tm=TQ)

    heads = (SB_HEADS, SB_DH)

    def from_feature_major(t):
        return jnp.transpose(t.reshape(bz, *heads, seq_p), (0, 3, 1, 2))[None]

    return (yp, ys.reshape(dec_b, dec_s, D_MODEL), st_p[None], st_s[None],
            from_feature_major(k_p), from_feature_major(v_p),
            k_s.reshape(1, dec_b, dec_s, *heads), v_s.reshape(1, dec_b, dec_s, *heads))
```

```python
import functools
import math

import jax
import jax.numpy as jnp
import numpy as np
from jax import lax
from jax.experimental import pallas as pl
from jax.experimental.pallas import tpu as pltpu

EPS = 1e-6
LOG2E = math.log2(math.e)

D_MODEL = 1024
GLA_HEADS = 4
GLA_DK = 128
GLA_DV = 256
GLA_KEY_W = GLA_HEADS * GLA_DK
GLA_VAL_W = GLA_HEADS * GLA_DV
GLA_RANK = 16
GLA_TAU = 16.0
GLA_CHUNK = 64
GLA_GROUP = 64
SB_HEADS = 16
SB_DH = 64
SB_W = SB_HEADS * SB_DH
PAGE = 128
LANES = 128
SUBLANES = 8
VMEM_LIMIT = 56 * 1024 * 1024

TQ = 256
TK = 128
QB = 512
GK = 512
Z2_MAX = 80.0

BF16 = jnp.bfloat16
F32 = jnp.float32


def _dot(a, b):
    return jnp.dot(a, b, preferred_element_type=F32)


def _dot_nt(a, b):
    return lax.dot_general(a, b, (((1,), (1,)), ((), ())), preferred_element_type=F32)


def _split2(x):
    hi = x.astype(BF16)
    lo = (x - hi.astype(F32)).astype(BF16)
    return hi, lo


def _log_sigmoid(x):
    return jnp.minimum(x, 0.0) - jnp.log(1.0 + jnp.exp(-jnp.abs(x)))


def _silu(x):
    return x / (1.0 + jnp.exp(-x))


def _rms_rows(x, gain):
    ms = jnp.mean(x * x, axis=-1, keepdims=True)
    return x * lax.rsqrt(ms + EPS) * gain


def _const_spec(shape):
    zeros = (0,) * len(shape)
    return pl.BlockSpec(shape, lambda *_: zeros, pipeline_mode=pl.Buffered(1))


def _gla_kernel(*refs, nb, tm, chunk, has_s0):
    if has_s0:
        (x_ref, s0_ref, gain_ref, wm_ref, wkt_ref, wac_ref, wact_ref, wup_ref, wupt_ref, ba_ref, bat_ref,
         on_ref, wout_ref, lcum_ref, mlater_ref, y_ref, sout_ref, s_scr, o_scr) = refs
    else:
        (x_ref, gain_ref, wm_ref, wkt_ref, wac_ref, wact_ref, wup_ref, wupt_ref, ba_ref, bat_ref,
         on_ref, wout_ref, lcum_ref, mlater_ref, y_ref, sout_ref, s_scr, o_scr) = refs
        s0_ref = None
    m = nb * tm
    step = pl.program_id(1)

    @pl.when(step == 0)
    def _():
        if has_s0:
            s_scr[...] = s0_ref[...]
        else:
            s_scr[...] = jnp.zeros_like(s_scr)

    x = x_ref[...].reshape(m, D_MODEL)
    h = _rms_rows(x, gain_ref[...]).astype(BF16)
    u = _dot(h, wm_ref[...])
    q = u[:, :GLA_KEY_W] * (GLA_DK ** -0.5)
    k = u[:, GLA_KEY_W:2 * GLA_KEY_W]
    v16 = u[:, 2 * GLA_KEY_W:2 * GLA_KEY_W + GLA_VAL_W].astype(BF16)
    gate = u[:, 2 * GLA_KEY_W + GLA_VAL_W:]

    code = _dot(h, wac_ref[...]).astype(BF16)
    la = _log_sigmoid(_dot(code, wup_ref[...]) + ba_ref[...]) * (1.0 / GLA_TAU)
    code_t = _dot_nt(wact_ref[...], h).astype(BF16)
    la_t = _log_sigmoid(_dot(wupt_ref[...], code_t) + bat_ref[...]) * (1.0 / GLA_TAU)
    k_t = _dot_nt(wkt_ref[...], h)

    la_hi, la_lo = _split2(la)
    b = _dot(lcum_ref[...], la_hi) + _dot(lcum_ref[...], la_lo)
    qd = (q * jnp.exp(b)).astype(BF16)
    kd = (k * jnp.exp(-b)).astype(BF16)
    lat_hi, lat_lo = _split2(la_t)
    rest_t = _dot(lat_hi, mlater_ref[...]) + _dot(lat_lo, mlater_ref[...])
    kl_t = k_t * jnp.exp(rest_t)

    lb = min(LANES, m)
    n_groups = m // GLA_GROUP
    n_sub = GLA_GROUP // chunk
    ri = lax.broadcasted_iota(jnp.int32, (GLA_GROUP, GLA_GROUP), 0)
    ci = lax.broadcasted_iota(jnp.int32, (GLA_GROUP, GLA_GROUP), 1)
    att_mask = ((ri // chunk) == (ci // chunk)) & (ci <= ri)

    for g in range(n_groups):
        r0 = g * GLA_GROUP
        rows = slice(r0, r0 + GLA_GROUP)
        blk = (r0 // lb) * lb
        cols = slice(blk, blk + lb)
        for hd in range(GLA_HEADS):
            fk = slice(hd * GLA_DK, (hd + 1) * GLA_DK)
            fv = slice(hd * GLA_DV, (hd + 1) * GLA_DV)
            qd_g = qd[rows, fk]
            att = jnp.where(att_mask, _dot_nt(qd_g, kd[rows, fk]), 0.0).astype(BF16)
            o = _dot(att, v16[rows, fv])
            for j in range(n_sub):
                sub = (r0 + j * chunk) // chunk
                si = (r0 + j * chunk) // tm
                s_old = s_scr[si, hd]
                o_st = _dot(qd_g, s_old.astype(BF16))
                if n_sub > 1:
                    rsel = lax.broadcasted_iota(jnp.int32, (GLA_GROUP, GLA_DV), 0)
                    o_st = jnp.where((rsel + r0) // chunk == sub, o_st, 0.0)
                o = o + o_st
                csel = (lax.broadcasted_iota(jnp.int32, (GLA_DK, lb), 1) + blk) // chunk == sub
                la_sel = jnp.where(csel, la_t[fk, cols], 0.0)
                kl_sel = jnp.where(csel, kl_t[fk, cols], 0.0).astype(BF16)
                decay = jnp.exp(jnp.sum(la_sel, axis=1, keepdims=True))
                s_scr[si, hd] = s_old * decay + _dot(kl_sel, v16[cols, fv])
            o_scr[rows, fv] = o

    y = x
    for hd in range(GLA_HEADS):
        fv = slice(hd * GLA_DV, (hd + 1) * GLA_DV)
        on = _rms_rows(o_scr[:, fv], on_ref[...])
        gated = (on * _silu(gate[:, fv])).astype(BF16)
        y = y + _dot(gated, wout_ref[fv, :])
    y_ref[...] = y.reshape(nb, tm, D_MODEL)

    @pl.when(step == pl.num_programs(1) - 1)
    def _():
        sout_ref[...] = s_scr[...]


def _block_structure(m, chunk):
    idx = np.arange(m)
    same = (idx[:, None] // chunk) == (idx[None, :] // chunk)
    lcum = same & (idx[None, :] <= idx[:, None])
    later = same & (idx[:, None] > idx[None, :])
    return jnp.asarray(lcum, BF16), jnp.asarray(later, BF16)


def _gla_layer(x, s0, gain, w_in, w_up, b_a, onorm, w_out, *, nb, tm, chunk):
    bz, length, _ = x.shape
    has_s0 = s0 is not None
    m = nb * tm
    assert bz % nb == 0 and length % tm == 0 and m % GLA_GROUP == 0
    assert chunk == GLA_GROUP or (chunk == tm and m == GLA_GROUP)
    wm = w_in[:, :2 * GLA_KEY_W + 2 * GLA_VAL_W].astype(BF16)
    wkt = w_in[:, GLA_KEY_W:2 * GLA_KEY_W].T.astype(BF16)
    wac = jnp.pad(w_in[:, 2 * GLA_KEY_W + 2 * GLA_VAL_W:], ((0, 0), (0, LANES - GLA_RANK))).astype(BF16)
    wup = jnp.pad(w_up, ((0, LANES - GLA_RANK), (0, 0))).astype(BF16)
    lcum, later = _block_structure(m, chunk)
    consts = [gain.reshape(1, D_MODEL), wm, wkt, wac, wac.T, wup, wup.T, b_a.reshape(1, GLA_KEY_W),
              b_a.reshape(GLA_KEY_W, 1), onorm.reshape(1, GLA_DV), w_out.astype(BF16), lcum, later]
    state_shape = (nb, GLA_HEADS, GLA_DK, GLA_DV)
    state_spec = pl.BlockSpec(state_shape, lambda b, l: (b, 0, 0, 0))
    x_spec = pl.BlockSpec((nb, tm, D_MODEL), lambda b, l: (b, l, 0))
    in_specs = [x_spec] + ([state_spec] if has_s0 else []) + [_const_spec(c.shape) for c in consts]
    args = [x] + ([s0] if has_s0 else []) + consts
    return pl.pallas_call(
        functools.partial(_gla_kernel, nb=nb, tm=tm, chunk=chunk, has_s0=has_s0),
        grid=(bz // nb, length // tm),
        in_specs=in_specs,
        out_specs=[x_spec, state_spec],
        out_shape=[jax.ShapeDtypeStruct(x.shape, F32),
                   jax.ShapeDtypeStruct((bz, GLA_HEADS, GLA_DK, GLA_DV), F32)],
        scratch_shapes=[pltpu.VMEM(state_shape, F32), pltpu.VMEM((m, GLA_VAL_W), F32)],
        compiler_params=pltpu.CompilerParams(dimension_semantics=("arbitrary", "arbitrary"),
                                             vmem_limit_bytes=VMEM_LIMIT),
        name="gla_layer_s0" if has_s0 else "gla_layer",
    )(*args)


def _key_permutation():
    p = np.arange(TK)
    return (TK // SUBLANES) * (p % SUBLANES) + p // SUBLANES


def _head_norm_t(t, gain_col, tokens):
    t3 = t.reshape(SB_HEADS, SB_DH, tokens)
    ms = jnp.mean(t3 * t3, axis=1, keepdims=True)
    return (t3 * lax.rsqrt(ms + EPS)).reshape(SB_W, tokens) * gain_col


def _sb_in_kernel(*refs, tm, prompt):
    if prompt:
        (x_ref, gain_ref, wt_ref, qg_ref, kg_ref, p_ref, pt_ref,
         k_ref, v_ref, qt_ref, kp_ref, vtp_ref, gt_ref) = refs
    else:
        (x_ref, gain_ref, wt_ref, qg_ref, kg_ref, k_ref, v_ref, q_ref, g_ref) = refs
    x = x_ref[0]
    h = _rms_rows(x, gain_ref[...]).astype(BF16)
    u_t = _dot_nt(wt_ref[...], h)
    q_t = _head_norm_t(u_t[:SB_W], qg_ref[...], tm) * (SB_DH ** -0.5 * (0.5 if prompt else LOG2E))
    k_t = _head_norm_t(u_t[SB_W:2 * SB_W], kg_ref[...], tm)
    v_t = u_t[2 * SB_W:3 * SB_W]
    g_t = u_t[3 * SB_W:]
    if prompt:
        k_ref[0] = k_t
        v_ref[0] = v_t
        qt_ref[...] = q_t.astype(BF16).reshape(SB_HEADS, SB_DH, tm)
        gt_ref[...] = g_t
        kp_ref[...] = _dot_nt(p_ref[...], k_t.astype(BF16)).astype(BF16).reshape(tm // TK, TK, SB_W)
        vp_t = _dot(v_t.astype(BF16), pt_ref[...])
        for j in range(tm // TK):
            vtp_ref[:, j] = vp_t[:, j * TK:(j + 1) * TK].astype(BF16).reshape(SB_HEADS, SB_DH, TK)
    else:
        k_ref[0] = k_t.T
        v_ref[0] = v_t.T
        q_ref[0] = q_t.T
        g_ref[0] = g_t.T


def _sb_in(x, gain, w_in, qn, kn, *, tm, prompt):
    bz, length, _ = x.shape
    assert length % tm == 0 and tm % TK == 0
    nt = length // tm
    wt = w_in.T.astype(BF16)
    qg = jnp.tile(qn, SB_HEADS).reshape(SB_W, 1)
    kg = jnp.tile(kn, SB_HEADS).reshape(SB_W, 1)
    consts = [gain.reshape(1, D_MODEL), wt, qg, kg]
    nat_spec = pl.BlockSpec((1, tm, SB_W), lambda b, l: (b, l, 0))
    nat_shape = jax.ShapeDtypeStruct((bz, length, SB_W), F32)
    if prompt:
        assert tm == TQ
        perm = np.concatenate([_key_permutation() + j * TK for j in range(tm // TK)])
        pt = np.zeros((tm, tm), np.float32)
        pt[perm, np.arange(tm)] = 1.0
        consts += [jnp.asarray(pt.T, BF16), jnp.asarray(pt, BF16)]
        t_spec = pl.BlockSpec((1, SB_W, tm), lambda b, l: (b, 0, l))
        t_shape = jax.ShapeDtypeStruct((bz, SB_W, length), F32)
        out_specs = [
            t_spec, t_spec,
            pl.BlockSpec((None, SB_HEADS, None, SB_DH, tm), lambda b, l: (b, 0, l, 0, 0)),
            pl.BlockSpec((None, tm // TK, TK, SB_W), lambda b, l: (b, l, 0, 0)),
            pl.BlockSpec((None, SB_HEADS, tm // TK, SB_DH, TK), lambda b, l: (b, 0, l, 0, 0)),
            pl.BlockSpec((None, None, SB_W, tm), lambda b, l: (b, l, 0, 0)),
        ]
        out_shape = [
            t_shape, t_shape,
            jax.ShapeDtypeStruct((bz, SB_HEADS, nt, SB_DH, tm), BF16),
            jax.ShapeDtypeStruct((bz, length // TK, TK, SB_W), BF16),
            jax.ShapeDtypeStruct((bz, SB_HEADS, length // TK, SB_DH, TK), BF16),
            jax.ShapeDtypeStruct((bz, nt, SB_W, tm), F32),
        ]
    else:
        out_specs = [nat_spec] * 4
        out_shape = [nat_shape] * 4
    return pl.pallas_call(
        functools.partial(_sb_in_kernel, tm=tm, prompt=prompt),
        grid=(bz, nt),
        in_specs=[nat_spec] + [_const_spec(c.shape) for c in consts],
        out_specs=out_specs,
        out_shape=out_shape,
        compiler_params=pltpu.CompilerParams(dimension_semantics=("arbitrary", "arbitrary"),
                                             vmem_limit_bytes=VMEM_LIMIT),
        name="sb_in_prompt" if prompt else "sb_in_sample",
    )(x, *consts)


def _shift_rows_up(x, s, rid):
    return jnp.where(rid < SUBLANES - s, pltpu.roll(x, SUBLANES - s, axis=0), 1.0)


def _sb_tile(u, carry, rid, mask=None):
    n = u.shape[-1]
    rho = 0.5 - 0.5 * jnp.tanh(u)
    if mask is not None:
        rho = jnp.where(mask, rho, 1.0)
    nv = TK // SUBLANES
    r3 = rho.reshape(nv, SUBLANES, n)
    run = [None] * nv
    run[nv - 1] = r3[nv - 1]
    for i in range(nv - 2, -1, -1):
        run[i] = r3[i] * run[i + 1]
    tot = run[0]
    later = _shift_rows_up(tot, 1, rid)
    later = later * _shift_rows_up(later, 1, rid)
    later = later * _shift_rows_up(later, 2, rid)
    later = later * _shift_rows_up(later, 4, rid)
    grp = later * carry
    run.append(jnp.ones_like(tot))
    diff = jnp.concatenate([run[i + 1] - run[i] for i in range(nv)], axis=0).astype(BF16)
    grp2 = jnp.concatenate([grp, grp], axis=0).astype(BF16)
    a = (diff.reshape(nv // 2, 2 * SUBLANES, n) * grp2[None]).reshape(TK, n)
    new_carry = jnp.broadcast_to((grp * tot)[0:1], (SUBLANES, n))
    return a, new_carry


def _sb_prompt_kernel(qt_ref, kp_ref, vtp_ref, bias_ref, ones_ref, ot_ref, u_scr, a_scr, *, nq):
    rid = lax.broadcasted_iota(jnp.int32, (SUBLANES, LANES), 0)
    prow = lax.broadcasted_iota(jnp.int32, (TK, LANES), 0)
    key_in_tile = (TK // SUBLANES) * (prow % SUBLANES) + prow // SUBLANES
    qlane = lax.broadcasted_iota(jnp.int32, (TK, LANES), 1)
    zeros_q = jnp.zeros((SB_DH, QB), BF16)
    tiles = GK // TK
    halves = QB // TQ
    n_groups = nq * (nq + 1) // 2

    heads = range(2)

    def scores(h2, qb, g, slot):
        q = jnp.concatenate([qt_ref[h2, halves * qb + i] for i in range(halves)], axis=1)
        parts = [q, zeros_q] if h2 == 0 else [zeros_q, q]
        qaug = jnp.concatenate(parts + [bias_ref[h2]], axis=0)
        keys = kp_ref[pl.ds(g * tiles, tiles)].reshape(GK, LANES)
        u_scr[slot, h2] = _dot(jnp.concatenate([keys, ones_ref[...]], axis=1), qaug)

    def weights(h2, slot, carry, diag):
        out = []
        for c in range(QB // LANES):
            cs = slice(c * LANES, (c + 1) * LANES)
            car = carry[:, cs]
            for t in range(tiles - 1, -1, -1):
                rows = slice(t * TK, (t + 1) * TK)
                lead = t * TK - c * LANES
                if diag and lead >= LANES:
                    a_scr[slot, h2, rows, cs] = jnp.zeros((TK, LANES), BF16)
                    continue
                mask = (key_in_tile + lead) < qlane if diag and lead > -TK else None
                a, car = _sb_tile(u_scr[slot, h2, rows, cs], car, rid, mask)
                a_scr[slot, h2, rows, cs] = a
            out.append(car)
        return jnp.concatenate(out, axis=1)

    def values(h2, qb, g, slot):
        vt = jnp.concatenate([vtp_ref[h2, g * tiles + t] for t in range(tiles)], axis=1)
        part = _dot(vt, a_scr[slot, h2])
        for i in range(halves):
            prev = jnp.where(g == qb, 0.0, ot_ref[h2, halves * qb + i])
            ot_ref[h2, halves * qb + i] = prev + part[:, i * TQ:(i + 1) * TQ]

    def step(qb, g, pqb, pg, carries, *, slot):
        last = g == 0
        nqb = jnp.minimum(jnp.where(last, qb + 1, qb), nq - 1)
        ng = jnp.where(last, nqb, g - 1)

        def body(diag):
            def run(carries):
                ones = jnp.ones((SUBLANES, QB), F32)
                carries = tuple(weights(h2, slot, ones if diag else carries[h2], diag) for h2 in heads)
                for h2 in heads:
                    values(h2, pqb, pg, 1 - slot)
                    scores(h2, nqb, ng, 1 - slot)
                return carries
            return run

        return nqb, ng, qb, g, lax.cond(g == qb, body(True), body(False), carries)

    zero = jnp.zeros((), jnp.int32)
    for h2 in heads:
        scores(h2, zero, zero, 0)
    a_scr[1] = jnp.zeros((2, GK, QB), BF16)
    ot_ref[...] = jnp.zeros_like(ot_ref)
    st = (zero, zero, zero, zero, (jnp.ones((SUBLANES, QB), F32),) * 2)
    first = n_groups % 2
    if first:
        st = step(*st, slot=0)
    st = lax.fori_loop(0, n_groups // 2, lambda _, s: step(*step(*s, slot=first), slot=1 - first), st)
    for h2 in heads:
        values(h2, st[2], st[3], (n_groups - 1) % 2)


def _sb_prompt(qt, kp, vtp, bias):
    bz, _, nq_t, _, _ = qt.shape
    nk = kp.shape[1]
    assert (nq_t * TQ) % QB == 0
    nq = nq_t * TQ // QB
    b2 = bias.astype(F32) * 0.5
    hi = b2.astype(BF16)
    mid = (b2 - hi.astype(F32)).astype(BF16)
    lo = (b2 - hi.astype(F32) - mid.astype(F32)).astype(BF16)
    rows = jnp.stack([hi, mid, lo], axis=1)
    bias_rows = jnp.zeros((SB_HEADS, LANES, QB), BF16).at[:, :3, :].set(
        jnp.broadcast_to(rows[:, :, None], (SB_HEADS, 3, QB)))
    bias_rows = bias_rows.reshape(SB_HEADS // 2, 2, LANES, QB)
    ones_cols = jnp.zeros((GK, LANES), BF16).at[:, :3].set(1.0)
    pairs = SB_HEADS // 2
    return pl.pallas_call(
        functools.partial(_sb_prompt_kernel, nq=nq),
        grid=(bz * pairs,),
        in_specs=[
            pl.BlockSpec((None, 2, nq_t, SB_DH, TQ), lambda g: (g // pairs, g % pairs, 0, 0, 0)),
            pl.BlockSpec((None, nk, TK, LANES), lambda g: (g // pairs, 0, 0, g % pairs)),
            pl.BlockSpec((None, 2, nk, SB_DH, TK), lambda g: (g // pairs, g % pairs, 0, 0, 0)),
            pl.BlockSpec((None, 2, LANES, QB), lambda g: (g % pairs, 0, 0, 0)),
            _const_spec((GK, LANES)),
        ],
        out_specs=pl.BlockSpec((None, 2, nq_t, SB_DH, TQ), lambda g: (g // pairs, g % pairs, 0, 0, 0)),
        out_shape=jax.ShapeDtypeStruct((bz, SB_HEADS, nq_t, SB_DH, TQ), F32),
        scratch_shapes=[pltpu.VMEM((2, 2, GK, QB), F32), pltpu.VMEM((2, 2, GK, QB), BF16)],
        compiler_params=pltpu.CompilerParams(dimension_semantics=("arbitrary",),
                                             vmem_limit_bytes=VMEM_LIMIT),
        name="sb_prompt",
    )(qt, kp, vtp, bias_rows, ones_cols)


def _sb_out_t_kernel(ot_ref, gt_ref, x_ref, w_ref, y_ref):
    o_t = ot_ref[...].reshape(SB_W, TQ)
    gated = (o_t * _silu(gt_ref[...])).T.astype(BF16)
    y_ref[0] = x_ref[0] + _dot(gated, w_ref[...])


def _sb_out_t(ot, gt, x, w_out):
    bz, _, nq, _, _ = ot.shape
    x_spec = pl.BlockSpec((1, TQ, D_MODEL), lambda b, l: (b, l, 0))
    return pl.pallas_call(
        _sb_out_t_kernel,
        grid=(bz, nq),
        in_specs=[
            pl.BlockSpec((None, SB_HEADS, None, SB_DH, TQ), lambda b, l: (b, 0, l, 0, 0)),
            pl.BlockSpec((None, None, SB_W, TQ), lambda b, l: (b, l, 0, 0)),
            x_spec,
            _const_spec((SB_W, D_MODEL)),
        ],
        out_specs=x_spec,
        out_shape=jax.ShapeDtypeStruct(x.shape, F32),
        compiler_params=pltpu.CompilerParams(dimension_semantics=("arbitrary", "arbitrary"),
                                             vmem_limit_bytes=VMEM_LIMIT),
        name="sb_out_prompt",
    )(ot, gt, x, w_out.astype(BF16))


PAGES_PER_STEP = 8


def _rev_cumprod_rows(m, carry):
    rows, n = m.shape
    rid = lax.broadcasted_iota(jnp.int32, (rows, n), 0) % SUBLANES
    p = m
    for s in (1, 2, 4):
        p = p * jnp.where(rid < SUBLANES - s, pltpu.roll(p, rows - s, axis=0), 1.0)
    nv = rows // SUBLANES
    p3 = p.reshape(nv, SUBLANES, n)
    out = [None] * nv
    for i in range(nv - 1, -1, -1):
        out[i] = p3[i] * carry
        carry = jnp.broadcast_to(out[i][0:1], (SUBLANES, n))
    return jnp.concatenate(out, axis=0), carry


def _sb_sample_kernel(pt_ref, q_ref, kn_ref, vn_ref, bias_ref, *refs, seq, n_steps):
    kpages = refs[:PAGES_PER_STEP]
    vpages = refs[PAGES_PER_STEP:2 * PAGES_PER_STEP]
    o_ref = refs[2 * PAGES_PER_STEP]
    qbd_scr, carry_scr, acc_scr, kt16_scr = refs[2 * PAGES_PER_STEP + 1:]
    step = pl.program_id(1)
    hq = SB_HEADS * seq

    def attend(u, vt16, carry, mask=None):
        w = jnp.exp2(jnp.minimum(u + bias_ref[...], Z2_MAX))
        if mask is not None:
            w = jnp.where(mask, w, 0.0)
        prod, carry = _rev_cumprod_rows(1.0 + w, carry)
        return _dot(vt16, (w / prod).astype(BF16)), carry

    @pl.when(step == 0)
    def _():
        q = q_ref[...]
        qrow = jnp.concatenate([q] * SB_HEADS, axis=0)
        rh = lax.broadcasted_iota(jnp.int32, (hq, SB_W), 0) // seq
        lh = lax.broadcasted_iota(jnp.int32, (hq, SB_W), 1) // SB_DH
        qbd = jnp.where(rh == lh, qrow, 0.0).astype(BF16)
        qbd_scr[...] = qbd
        pad = jnp.zeros((PAGE - seq, SB_W), F32)
        k_new = jnp.concatenate([kn_ref[...], pad], axis=0).astype(BF16)
        vt_new = jnp.concatenate([vn_ref[...], pad], axis=0).T.astype(BF16)
        lane = lax.broadcasted_iota(jnp.int32, (PAGE, hq), 1)
        krow = lax.broadcasted_iota(jnp.int32, (PAGE, hq), 0)
        o, carry = attend(_dot_nt(k_new, qbd), vt_new, jnp.ones((SUBLANES, hq), F32), krow < (lane % seq))
        acc_scr[...] = o
        carry_scr[...] = carry

    for i in range(PAGES_PER_STEP):
        kt16_scr[:, i * PAGE:(i + 1) * PAGE] = kpages[i][...].astype(BF16)
    z_all = _dot(qbd_scr[...], kt16_scr[...])
    carry = carry_scr[...]
    acc = acc_scr[...]
    for i in range(PAGES_PER_STEP):
        o, carry = attend(z_all[:, i * PAGE:(i + 1) * PAGE].T, vpages[i][...].astype(BF16), carry)
        acc = acc + o
    acc_scr[...] = acc
    carry_scr[...] = carry

    @pl.when(step == n_steps - 1)
    def _():
        o3 = acc.T.reshape(SB_HEADS, seq, SB_W)
        hsel = lax.broadcasted_iota(jnp.int32, (SB_HEADS, seq, SB_W), 0)
        lsel = lax.broadcasted_iota(jnp.int32, (SB_HEADS, seq, SB_W), 2) // SB_DH
        o_ref[...] = jnp.sum(jnp.where(hsel == lsel, o3, 0.0), axis=0)


def _sb_sample(q, k_new, v_new, cache_kt, cache_vt, page_table, bias):
    bz, seq, _ = q.shape
    n_pages = page_table.shape[1]
    assert n_pages % PAGES_PER_STEP == 0 and seq == SUBLANES
    n_steps = n_pages // PAGES_PER_STEP
    bias_lane = jnp.repeat(bias.astype(F32) * LOG2E, seq).reshape(1, SB_HEADS * seq)
    tok_spec = pl.BlockSpec((None, seq, SB_W), lambda b, s, pt: (b, 0, 0))

    def page_spec(i):
        def index(b, s, pt):
            return (pt[b * n_pages + n_pages - 1 - (s * PAGES_PER_STEP + i)], 0, 0)
        return pl.BlockSpec((None, SB_W, PAGE), index)

    pages = [page_spec(i) for i in range(PAGES_PER_STEP)]
    hq = SB_HEADS * seq
    return pl.pallas_call(
        functools.partial(_sb_sample_kernel, seq=seq, n_steps=n_steps),
        grid_spec=pltpu.PrefetchScalarGridSpec(
            num_scalar_prefetch=1,
            grid=(bz, n_steps),
            in_specs=[tok_spec, tok_spec, tok_spec, pl.BlockSpec((1, hq), lambda b, s, pt: (0, 0))] + pages + pages,
            out_specs=tok_spec,
            scratch_shapes=[pltpu.VMEM((hq, SB_W), BF16), pltpu.VMEM((SUBLANES, hq), F32),
                            pltpu.VMEM((SB_W, hq), F32), pltpu.VMEM((SB_W, PAGES_PER_STEP * PAGE), BF16)],
        ),
        out_shape=jax.ShapeDtypeStruct((bz, seq, SB_W), F32),
        compiler_params=pltpu.CompilerParams(dimension_semantics=("arbitrary", "arbitrary"),
                                             vmem_limit_bytes=VMEM_LIMIT),
        name="sb_sample",
    )(page_table.reshape(-1), q, k_new, v_new, bias_lane,
      *([cache_kt] * PAGES_PER_STEP), *([cache_vt] * PAGES_PER_STEP))


def _sb_out_kernel(o_ref, g_ref, x_ref, w_ref, y_ref):
    gated = (o_ref[...] * _silu(g_ref[...])).astype(BF16)
    y_ref[...] = x_ref[...] + _dot(gated, w_ref[...])


def _sb_out(o, g, x, w_out, *, tm):
    n = o.shape[0]
    spec = pl.BlockSpec((tm, D_MODEL), lambda i: (i, 0))
    return pl.pallas_call(
        _sb_out_kernel,
        grid=(n // tm,),
        in_specs=[spec, spec, spec, _const_spec((SB_W, D_MODEL))],
        out_specs=spec,
        out_shape=jax.ShapeDtypeStruct((n, D_MODEL), F32),
        compiler_params=pltpu.CompilerParams(dimension_semantics=("arbitrary",), vmem_limit_bytes=VMEM_LIMIT),
        name="sb_out_sample",
    )(o, g, x, w_out.astype(BF16))


def kernel(x_prompt, x_sample, state_gla, cache_k, cache_v, page_table, norm_gain, w_in_a, w_alpha_up, b_alpha, onorm_a, w_out_a, w_in_b, qnorm_b, knorm_b, sb_bias, w_out_b):
    bz, seq_p, _ = x_prompt.shape
    dec_b, dec_s, _ = x_sample.shape

    gla = functools.partial(_gla_layer, gain=norm_gain[0], w_in=w_in_a[0], w_up=w_alpha_up[0], b_a=b_alpha[0],
                            onorm=onorm_a[0], w_out=w_out_a[0])
    xp, st_p = gla(x_prompt, None, nb=1, tm=4 * GLA_CHUNK, chunk=GLA_CHUNK)
    xs, st_s = gla(x_sample, state_gla[0], nb=GLA_GROUP // dec_s, tm=dec_s, chunk=dec_s)

    k_p, v_p, qt, kp, vtp, gt = _sb_in(xp, norm_gain[1], w_in_b[0], qnorm_b[0], knorm_b[0], tm=TQ, prompt=True)
    ot = _sb_prompt(qt, kp, vtp, sb_bias[0])
    yp = _sb_out_t(ot, gt, xp, w_out_b[0])

    xs_flat = xs.reshape(1, dec_b * dec_s, D_MODEL)
    k_s, v_s, q_s, g_s = _sb_in(xs_flat, norm_gain[1], w_in_b[0], qnorm_b[0], knorm_b[0], tm=TQ, prompt=False)
    per_seq = (dec_b, dec_s, SB_W)
    n_phys = cache_k.shape[1]
    cache_kt = jnp.transpose(cache_k[0], (0, 2, 3, 1)).reshape(n_phys, SB_W, PAGE)
    cache_vt = jnp.transpose(cache_v[0], (0, 2, 3, 1)).reshape(n_phys, SB_W, PAGE)
    o_s = _sb_sample(q_s.reshape(per_seq), k_s.reshape(per_seq), v_s.reshape(per_seq),
                     cache_kt, cache_vt, page_table, sb_bias[0])
    ys = _sb_out(o_s.reshape(dec_b * dec_s, SB_W), g_s[0], xs_flat[0], w_out_b[0], tm=TQ)

    heads = (SB_HEADS, SB_DH)

    def from_feature_major(t):
        return jnp.transpose(t.reshape(bz, *heads, seq_p), (0, 3, 1, 2))[None]

    return (yp, ys.reshape(dec_b, dec_s, D_MODEL), st_p[None], st_s[None],
            from_feature_major(k_p), from_feature_major(v_p),
            k_s.reshape(1, dec_b, dec_s, *heads), v_s.reshape(1, dec_b, dec_s, *heads))
```

```python
import functools
import math

import jax
import jax.numpy as jnp
import numpy as np
from jax import lax
from jax.experimental import pallas as pl
from jax.experimental.pallas import tpu as pltpu

EPS = 1e-6
LOG2E = math.log2(math.e)

D_MODEL = 1024
GLA_HEADS = 4
GLA_DK = 128
GLA_DV = 256
GLA_KEY_W = GLA_HEADS * GLA_DK
GLA_VAL_W = GLA_HEADS * GLA_DV
GLA_RANK = 16
GLA_TAU = 16.0
GLA_CHUNK = 64
GLA_GROUP = 64
SB_HEADS = 16
SB_DH = 64
SB_W = SB_HEADS * SB_DH
PAGE = 128
LANES = 128
SUBLANES = 8
VMEM_LIMIT = 56 * 1024 * 1024

TQ = 256
TK = 128
QB = 512
GK = 512
Z2_MAX = 80.0

BF16 = jnp.bfloat16
F32 = jnp.float32


def _dot(a, b):
    return jnp.dot(a, b, preferred_element_type=F32)


def _dot_nt(a, b):
    return lax.dot_general(a, b, (((1,), (1,)), ((), ())), preferred_element_type=F32)


def _split2(x):
    hi = x.astype(BF16)
    lo = (x - hi.astype(F32)).astype(BF16)
    return hi, lo


def _log_sigmoid(x):
    return jnp.minimum(x, 0.0) - jnp.log(1.0 + jnp.exp(-jnp.abs(x)))


def _silu(x):
    return x / (1.0 + jnp.exp(-x))


def _rms_rows(x, gain):
    ms = jnp.mean(x * x, axis=-1, keepdims=True)
    return x * lax.rsqrt(ms + EPS) * gain


def _const_spec(shape):
    zeros = (0,) * len(shape)
    return pl.BlockSpec(shape, lambda *_: zeros, pipeline_mode=pl.Buffered(1))


def _gla_kernel(*refs, nb, tm, chunk, has_s0):
    if has_s0:
        (x_ref, s0_ref, gain_ref, wm_ref, wkt_ref, wac_ref, wact_ref, wup_ref, wupt_ref, ba_ref, bat_ref,
         on_ref, wout_ref, lcum_ref, mlater_ref, y_ref, sout_ref, s_scr, o_scr) = refs
    else:
        (x_ref, gain_ref, wm_ref, wkt_ref, wac_ref, wact_ref, wup_ref, wupt_ref, ba_ref, bat_ref,
         on_ref, wout_ref, lcum_ref, mlater_ref, y_ref, sout_ref, s_scr, o_scr) = refs
        s0_ref = None
    m = nb * tm
    step = pl.program_id(1)

    @pl.when(step == 0)
    def _():
        if has_s0:
            s_scr[...] = s0_ref[...]
        else:
            s_scr[...] = jnp.zeros_like(s_scr)

    x = x_ref[...].reshape(m, D_MODEL)
    h = _rms_rows(x, gain_ref[...]).astype(BF16)
    u = _dot(h, wm_ref[...])
    q = u[:, :GLA_KEY_W] * (GLA_DK ** -0.5)
    k = u[:, GLA_KEY_W:2 * GLA_KEY_W]
    v16 = u[:, 2 * GLA_KEY_W:2 * GLA_KEY_W + GLA_VAL_W].astype(BF16)
    gate = u[:, 2 * GLA_KEY_W + GLA_VAL_W:]

    code = _dot(h, wac_ref[...]).astype(BF16)
    la = _log_sigmoid(_dot(code, wup_ref[...]) + ba_ref[...]) * (1.0 / GLA_TAU)
    code_t = _dot_nt(wact_ref[...], h).astype(BF16)
    la_t = _log_sigmoid(_dot(wupt_ref[...], code_t) + bat_ref[...]) * (1.0 / GLA_TAU)
    k_t = k.T if m % LANES == 0 else _dot_nt(wkt_ref[...], h)

    la_hi, la_lo = _split2(la)
    b = _dot(lcum_ref[...], la_hi) + _dot(lcum_ref[...], la_lo)
    qd = (q * jnp.exp(b)).astype(BF16)
    kd = (k * jnp.exp(-b)).astype(BF16)
    lat_hi, lat_lo = _split2(la_t)
    rest_t = _dot(lat_hi, mlater_ref[...]) + _dot(lat_lo, mlater_ref[...])
    kl_t = k_t * jnp.exp(rest_t)

    lb = min(LANES, m)
    n_groups = m // GLA_GROUP
    n_sub = GLA_GROUP // chunk
    ri = lax.broadcasted_iota(jnp.int32, (GLA_GROUP, GLA_GROUP), 0)
    ci = lax.broadcasted_iota(jnp.int32, (GLA_GROUP, GLA_GROUP), 1)
    att_mask = ((ri // chunk) == (ci // chunk)) & (ci <= ri)

    for g in range(n_groups):
        r0 = g * GLA_GROUP
        rows = slice(r0, r0 + GLA_GROUP)
        blk = (r0 // lb) * lb
        cols = slice(blk, blk + lb)
        for hd in range(GLA_HEADS):
            fk = slice(hd * GLA_DK, (hd + 1) * GLA_DK)
            fv = slice(hd * GLA_DV, (hd + 1) * GLA_DV)
            qd_g = qd[rows, fk]
            att = jnp.where(att_mask, _dot_nt(qd_g, kd[rows, fk]), 0.0).astype(BF16)
            o = _dot(att, v16[rows, fv])
            for j in range(n_sub):
                sub = (r0 + j * chunk) // chunk
                si = (r0 + j * chunk) // tm
                s_old = s_scr[si, hd]
                o_st = _dot(qd_g, s_old.astype(BF16))
                if n_sub > 1:
                    rsel = lax.broadcasted_iota(jnp.int32, (GLA_GROUP, GLA_DV), 0)
                    o_st = jnp.where((rsel + r0) // chunk == sub, o_st, 0.0)
                o = o + o_st
                csel = (lax.broadcasted_iota(jnp.int32, (GLA_DK, lb), 1) + blk) // chunk == sub
                la_sel = jnp.where(csel, la_t[fk, cols], 0.0)
                kl_sel = jnp.where(csel, kl_t[fk, cols], 0.0).astype(BF16)
                decay = jnp.exp(jnp.sum(la_sel, axis=1, keepdims=True))
                s_scr[si, hd] = s_old * decay + _dot(kl_sel, v16[cols, fv])
            o_scr[rows, fv] = o

    y = x
    for hd in range(GLA_HEADS):
        fv = slice(hd * GLA_DV, (hd + 1) * GLA_DV)
        on = _rms_rows(o_scr[:, fv], on_ref[...])
        gated = (on * _silu(gate[:, fv])).astype(BF16)
        y = y + _dot(gated, wout_ref[fv, :])
    y_ref[...] = y.reshape(nb, tm, D_MODEL)

    @pl.when(step == pl.num_programs(1) - 1)
    def _():
        sout_ref[...] = s_scr[...]


def _block_structure(m, chunk):
    idx = np.arange(m)
    same = (idx[:, None] // chunk) == (idx[None, :] // chunk)
    lcum = same & (idx[None, :] <= idx[:, None])
    later = same & (idx[:, None] > idx[None, :])
    return jnp.asarray(lcum, BF16), jnp.asarray(later, BF16)


def _gla_layer(x, s0, gain, w_in, w_up, b_a, onorm, w_out, *, nb, tm, chunk):
    bz, length, _ = x.shape
    has_s0 = s0 is not None
    m = nb * tm
    assert bz % nb == 0 and length % tm == 0 and m % GLA_GROUP == 0
    assert chunk == GLA_GROUP or (chunk == tm and m == GLA_GROUP)
    wm = w_in[:, :2 * GLA_KEY_W + 2 * GLA_VAL_W].astype(BF16)
    wkt = w_in[:, GLA_KEY_W:2 * GLA_KEY_W].T.astype(BF16)
    wac = jnp.pad(w_in[:, 2 * GLA_KEY_W + 2 * GLA_VAL_W:], ((0, 0), (0, LANES - GLA_RANK))).astype(BF16)
    wup = jnp.pad(w_up, ((0, LANES - GLA_RANK), (0, 0))).astype(BF16)
    lcum, later = _block_structure(m, chunk)
    consts = [gain.reshape(1, D_MODEL), wm, wkt, wac, wac.T, wup, wup.T, b_a.reshape(1, GLA_KEY_W),
              b_a.reshape(GLA_KEY_W, 1), onorm.reshape(1, GLA_DV), w_out.astype(BF16), lcum, later]
    state_shape = (nb, GLA_HEADS, GLA_DK, GLA_DV)
    state_spec = pl.BlockSpec(state_shape, lambda b, l: (b, 0, 0, 0))
    x_spec = pl.BlockSpec((nb, tm, D_MODEL), lambda b, l: (b, l, 0))
    in_specs = [x_spec] + ([state_spec] if has_s0 else []) + [_const_spec(c.shape) for c in consts]
    args = [x] + ([s0] if has_s0 else []) + consts
    return pl.pallas_call(
        functools.partial(_gla_kernel, nb=nb, tm=tm, chunk=chunk, has_s0=has_s0),
        grid=(bz // nb, length // tm),
        in_specs=in_specs,
        out_specs=[x_spec, state_spec],
        out_shape=[jax.ShapeDtypeStruct(x.shape, F32),
                   jax.ShapeDtypeStruct((bz, GLA_HEADS, GLA_DK, GLA_DV), F32)],
        scratch_shapes=[pltpu.VMEM(state_shape, F32), pltpu.VMEM((m, GLA_VAL_W), F32)],
        compiler_params=pltpu.CompilerParams(dimension_semantics=("arbitrary", "arbitrary"),
                                             vmem_limit_bytes=VMEM_LIMIT),
        name="gla_layer_s0" if has_s0 else "gla_layer",
    )(*args)


def _key_permutation():
    p = np.arange(TK)
    return (TK // SUBLANES) * (p % SUBLANES) + p // SUBLANES


def _head_norm_t(t, gain_col, tokens):
    t3 = t.reshape(SB_HEADS, SB_DH, tokens)
    ms = jnp.mean(t3 * t3, axis=1, keepdims=True)
    return (t3 * lax.rsqrt(ms + EPS)).reshape(SB_W, tokens) * gain_col


def _sb_in_kernel(*refs, tm, prompt):
    if prompt:
        (x_ref, gain_ref, wt_ref, qg_ref, kg_ref, p_ref, pt_ref,
         k_ref, v_ref, qt_ref, kp_ref, vtp_ref, gt_ref) = refs
    else:
        (x_ref, gain_ref, wt_ref, qg_ref, kg_ref, k_ref, v_ref, q_ref, g_ref) = refs
    x = x_ref[0]
    h = _rms_rows(x, gain_ref[...]).astype(BF16)
    u_t = _dot_nt(wt_ref[...], h)
    q_t = _head_norm_t(u_t[:SB_W], qg_ref[...], tm) * (SB_DH ** -0.5 * (0.5 if prompt else LOG2E))
    k_t = _head_norm_t(u_t[SB_W:2 * SB_W], kg_ref[...], tm)
    v_t = u_t[2 * SB_W:3 * SB_W]
    g_t = u_t[3 * SB_W:]
    if prompt:
        k_ref[0] = k_t
        v_ref[0] = v_t
        qt_ref[...] = q_t.astype(BF16).reshape(SB_HEADS, SB_DH, tm)
        gt_ref[...] = g_t.astype(BF16)
        kp_ref[...] = _dot_nt(p_ref[...], k_t.astype(BF16)).astype(BF16).reshape(tm // TK, TK, SB_W)
        vp_t = _dot(v_t.astype(BF16), pt_ref[...])
        for j in range(tm // TK):
            vtp_ref[:, j] = vp_t[:, j * TK:(j + 1) * TK].astype(BF16).reshape(SB_HEADS, SB_DH, TK)
    else:
        k_ref[0] = k_t.T
        v_ref[0] = v_t.T
        q_ref[0] = q_t.T
        g_ref[0] = g_t.T


def _sb_in(x, gain, w_in, qn, kn, *, tm, prompt):
    bz, length, _ = x.shape
    assert length % tm == 0 and tm % TK == 0
    nt = length // tm
    wt = w_in.T.astype(BF16)
    qg = jnp.tile(qn, SB_HEADS).reshape(SB_W, 1)
    kg = jnp.tile(kn, SB_HEADS).reshape(SB_W, 1)
    consts = [gain.reshape(1, D_MODEL), wt, qg, kg]
    nat_spec = pl.BlockSpec((1, tm, SB_W), lambda b, l: (b, l, 0))
    nat_shape = jax.ShapeDtypeStruct((bz, length, SB_W), F32)
    if prompt:
        assert tm == TQ
        perm = np.concatenate([_key_permutation() + j * TK for j in range(tm // TK)])
        pt = np.zeros((tm, tm), np.float32)
        pt[perm, np.arange(tm)] = 1.0
        consts += [jnp.asarray(pt.T, BF16), jnp.asarray(pt, BF16)]
        t_spec = pl.BlockSpec((1, SB_W, tm), lambda b, l: (b, 0, l))
        t_shape = jax.ShapeDtypeStruct((bz, SB_W, length), F32)
        out_specs = [
            t_spec, t_spec,
            pl.BlockSpec((None, SB_HEADS, None, SB_DH, tm), lambda b, l: (b, 0, l, 0, 0)),
            pl.BlockSpec((None, tm // TK, TK, SB_W), lambda b, l: (b, l, 0, 0)),
            pl.BlockSpec((None, SB_HEADS, tm // TK, SB_DH, TK), lambda b, l: (b, 0, l, 0, 0)),
            pl.BlockSpec((None, None, SB_W, tm), lambda b, l: (b, l, 0, 0)),
        ]
        out_shape = [
            t_shape, t_shape,
            jax.ShapeDtypeStruct((bz, SB_HEADS, nt, SB_DH, tm), BF16),
            jax.ShapeDtypeStruct((bz, length // TK, TK, SB_W), BF16),
            jax.ShapeDtypeStruct((bz, SB_HEADS, length // TK, SB_DH, TK), BF16),
            jax.ShapeDtypeStruct((bz, nt, SB_W, tm), BF16),
        ]
    else:
        out_specs = [nat_spec] * 4
        out_shape = [nat_shape] * 4
    return pl.pallas_call(
        functools.partial(_sb_in_kernel, tm=tm, prompt=prompt),
        grid=(bz, nt),
        in_specs=[nat_spec] + [_const_spec(c.shape) for c in consts],
        out_specs=out_specs,
        out_shape=out_shape,
        compiler_params=pltpu.CompilerParams(dimension_semantics=("arbitrary", "arbitrary"),
                                             vmem_limit_bytes=VMEM_LIMIT),
        name="sb_in_prompt" if prompt else "sb_in_sample",
    )(x, *consts)


def _shift_rows_up(x, s, rid):
    return jnp.where(rid < SUBLANES - s, pltpu.roll(x, SUBLANES - s, axis=0), 1.0)


def _sb_tile(u, carry, rid, mask=None):
    n = u.shape[-1]
    rho = 0.5 - 0.5 * jnp.tanh(u)
    if mask is not None:
        rho = jnp.where(mask, rho, 1.0)
    nv = TK // SUBLANES
    r3 = rho.reshape(nv, SUBLANES, n)
    run = [None] * nv
    run[nv - 1] = r3[nv - 1]
    for i in range(nv - 2, -1, -1):
        run[i] = r3[i] * run[i + 1]
    tot = run[0]
    later = _shift_rows_up(tot, 1, rid)
    later = later * _shift_rows_up(later, 1, rid)
    later = later * _shift_rows_up(later, 2, rid)
    later = later * _shift_rows_up(later, 4, rid)
    grp = later * carry
    run.append(jnp.ones_like(tot))
    diff = jnp.concatenate([run[i + 1] - run[i] for i in range(nv)], axis=0).astype(BF16)
    grp2 = jnp.concatenate([grp, grp], axis=0).astype(BF16)
    a = (diff.reshape(nv // 2, 2 * SUBLANES, n) * grp2[None]).reshape(TK, n)
    new_carry = jnp.broadcast_to((grp * tot)[0:1], (SUBLANES, n))
    return a, new_carry


def _sb_prompt_kernel(qt_ref, kp_ref, vtp_ref, bias_ref, ones_ref, ot_ref, u_scr, a_scr, *, nq):
    rid = lax.broadcasted_iota(jnp.int32, (SUBLANES, LANES), 0)
    prow = lax.broadcasted_iota(jnp.int32, (TK, LANES), 0)
    key_in_tile = (TK // SUBLANES) * (prow % SUBLANES) + prow // SUBLANES
    qlane = lax.broadcasted_iota(jnp.int32, (TK, LANES), 1)
    zeros_q = jnp.zeros((SB_DH, QB), BF16)
    tiles = GK // TK
    halves = QB // TQ
    n_groups = nq * (nq + 1) // 2

    heads = range(2)

    def scores(h2, qb, g, slot):
        q = jnp.concatenate([qt_ref[h2, halves * qb + i] for i in range(halves)], axis=1)
        parts = [q, zeros_q] if h2 == 0 else [zeros_q, q]
        qaug = jnp.concatenate(parts + [bias_ref[h2]], axis=0)
        keys = kp_ref[pl.ds(g * tiles, tiles)].reshape(GK, LANES)
        u_scr[slot, h2] = _dot(jnp.concatenate([keys, ones_ref[...]], axis=1), qaug)

    def weights(h2, slot, carry, diag):
        out = []
        for c in range(QB // LANES):
            cs = slice(c * LANES, (c + 1) * LANES)
            car = carry[:, cs]
            for t in range(tiles - 1, -1, -1):
                rows = slice(t * TK, (t + 1) * TK)
                lead = t * TK - c * LANES
                if diag and lead >= LANES:
                    a_scr[slot, h2, rows, cs] = jnp.zeros((TK, LANES), BF16)
                    continue
                mask = (key_in_tile + lead) < qlane if diag and lead > -TK else None
                a, car = _sb_tile(u_scr[slot, h2, rows, cs], car, rid, mask)
                a_scr[slot, h2, rows, cs] = a
            out.append(car)
        return jnp.concatenate(out, axis=1)

    def values(h2, qb, g, slot):
        vt = jnp.concatenate([vtp_ref[h2, g * tiles + t] for t in range(tiles)], axis=1)
        part = _dot(vt, a_scr[slot, h2])
        for i in range(halves):
            prev = jnp.where(g == qb, 0.0, ot_ref[h2, halves * qb + i])
            ot_ref[h2, halves * qb + i] = prev + part[:, i * TQ:(i + 1) * TQ]

    def step(qb, g, pqb, pg, carries, *, slot):
        last = g == 0
        nqb = jnp.minimum(jnp.where(last, qb + 1, qb), nq - 1)
        ng = jnp.where(last, nqb, g - 1)

        def body(diag):
            def run(carries):
                ones = jnp.ones((SUBLANES, QB), F32)
                carries = tuple(weights(h2, slot, ones if diag else carries[h2], diag) for h2 in heads)
                for h2 in heads:
                    values(h2, pqb, pg, 1 - slot)
                    scores(h2, nqb, ng, 1 - slot)
                return carries
            return run

        return nqb, ng, qb, g, lax.cond(g == qb, body(True), body(False), carries)

    zero = jnp.zeros((), jnp.int32)
    for h2 in heads:
        scores(h2, zero, zero, 0)
    a_scr[1] = jnp.zeros((2, GK, QB), BF16)
    ot_ref[...] = jnp.zeros_like(ot_ref)
    st = (zero, zero, zero, zero, (jnp.ones((SUBLANES, QB), F32),) * 2)
    first = n_groups % 2
    if first:
        st = step(*st, slot=0)
    st = lax.fori_loop(0, n_groups // 2, lambda _, s: step(*step(*s, slot=first), slot=1 - first), st)
    for h2 in heads:
        values(h2, st[2], st[3], (n_groups - 1) % 2)


def _sb_prompt(qt, kp, vtp, bias):
    bz, _, nq_t, _, _ = qt.shape
    nk = kp.shape[1]
    assert (nq_t * TQ) % QB == 0
    nq = nq_t * TQ // QB
    b2 = bias.astype(F32) * 0.5
    hi = b2.astype(BF16)
    mid = (b2 - hi.astype(F32)).astype(BF16)
    lo = (b2 - hi.astype(F32) - mid.astype(F32)).astype(BF16)
    rows = jnp.stack([hi, mid, lo], axis=1)
    bias_rows = jnp.zeros((SB_HEADS, LANES, QB), BF16).at[:, :3, :].set(
        jnp.broadcast_to(rows[:, :, None], (SB_HEADS, 3, QB)))
    bias_rows = bias_rows.reshape(SB_HEADS // 2, 2, LANES, QB)
    ones_cols = jnp.zeros((GK, LANES), BF16).at[:, :3].set(1.0)
    pairs = SB_HEADS // 2
    return pl.pallas_call(
        functools.partial(_sb_prompt_kernel, nq=nq),
        grid=(bz * pairs,),
        in_specs=[
            pl.BlockSpec((None, 2, nq_t, SB_DH, TQ), lambda g: (g // pairs, g % pairs, 0, 0, 0)),
            pl.BlockSpec((None, nk, TK, LANES), lambda g: (g // pairs, 0, 0, g % pairs)),
            pl.BlockSpec((None, 2, nk, SB_DH, TK), lambda g: (g // pairs, g % pairs, 0, 0, 0)),
            pl.BlockSpec((None, 2, LANES, QB), lambda g: (g % pairs, 0, 0, 0)),
            _const_spec((GK, LANES)),
        ],
        out_specs=pl.BlockSpec((None, 2, nq_t, SB_DH, TQ), lambda g: (g // pairs, g % pairs, 0, 0, 0)),
        out_shape=jax.ShapeDtypeStruct((bz, SB_HEADS, nq_t, SB_DH, TQ), F32),
        scratch_shapes=[pltpu.VMEM((2, 2, GK, QB), F32), pltpu.VMEM((2, 2, GK, QB), BF16)],
        compiler_params=pltpu.CompilerParams(dimension_semantics=("arbitrary",),
                                             vmem_limit_bytes=VMEM_LIMIT),
        name="sb_prompt",
    )(qt, kp, vtp, bias_rows, ones_cols)


def _sb_out_t_kernel(ot_ref, gt_ref, x_ref, w_ref, y_ref):
    o_t = ot_ref[...].reshape(SB_W, TQ)
    gated = (o_t * _silu(gt_ref[...].astype(F32))).T.astype(BF16)
    y_ref[0] = x_ref[0] + _dot(gated, w_ref[...])


def _sb_out_t(ot, gt, x, w_out):
    bz, _, nq, _, _ = ot.shape
    x_spec = pl.BlockSpec((1, TQ, D_MODEL), lambda b, l: (b, l, 0))
    return pl.pallas_call(
        _sb_out_t_kernel,
        grid=(bz, nq),
        in_specs=[
            pl.BlockSpec((None, SB_HEADS, None, SB_DH, TQ), lambda b, l: (b, 0, l, 0, 0)),
            pl.BlockSpec((None, None, SB_W, TQ), lambda b, l: (b, l, 0, 0)),
            x_spec,
            _const_spec((SB_W, D_MODEL)),
        ],
        out_specs=x_spec,
        out_shape=jax.ShapeDtypeStruct(x.shape, F32),
        compiler_params=pltpu.CompilerParams(dimension_semantics=("arbitrary", "arbitrary"),
                                             vmem_limit_bytes=VMEM_LIMIT),
        name="sb_out_prompt",
    )(ot, gt, x, w_out.astype(BF16))


PAGES_PER_STEP = 8


def _rev_cumprod_rows(m, carry):
    rows, n = m.shape
    rid = lax.broadcasted_iota(jnp.int32, (rows, n), 0) % SUBLANES
    p = m
    for s in (1, 2, 4):
        p = p * jnp.where(rid < SUBLANES - s, pltpu.roll(p, rows - s, axis=0), 1.0)
    nv = rows // SUBLANES
    p3 = p.reshape(nv, SUBLANES, n)
    out = [None] * nv
    for i in range(nv - 1, -1, -1):
        out[i] = p3[i] * carry
        carry = jnp.broadcast_to(out[i][0:1], (SUBLANES, n))
    return jnp.concatenate(out, axis=0), carry


def _sb_sample_kernel(pt_ref, q_ref, kn_ref, vn_ref, bias_ref, *refs, seq, n_steps):
    kpages = refs[:PAGES_PER_STEP]
    vpages = refs[PAGES_PER_STEP:2 * PAGES_PER_STEP]
    o_ref = refs[2 * PAGES_PER_STEP]
    qbd_scr, carry_scr, acc_scr, kt16_scr = refs[2 * PAGES_PER_STEP + 1:]
    step = pl.program_id(1)
    hq = SB_HEADS * seq

    def attend(u, carry, mask=None):
        w = jnp.exp2(jnp.minimum(u + bias_ref[...], Z2_MAX))
        if mask is not None:
            w = jnp.where(mask, w, 0.0)
        prod, carry = _rev_cumprod_rows(1.0 + w, carry)
        return (w / prod).T.astype(BF16), carry

    @pl.when(step == 0)
    def _():
        q = q_ref[...]
        qrow = jnp.concatenate([q] * SB_HEADS, axis=0)
        rh = lax.broadcasted_iota(jnp.int32, (hq, SB_W), 0) // seq
        lh = lax.broadcasted_iota(jnp.int32, (hq, SB_W), 1) // SB_DH
        qbd = jnp.where(rh == lh, qrow, 0.0).astype(BF16)
        qbd_scr[...] = qbd
        pad = jnp.zeros((PAGE - seq, SB_W), F32)
        k_new = jnp.concatenate([kn_ref[...], pad], axis=0).astype(BF16)
        v_new = jnp.concatenate([vn_ref[...], pad], axis=0).astype(BF16)
        lane = lax.broadcasted_iota(jnp.int32, (PAGE, hq), 1)
        krow = lax.broadcasted_iota(jnp.int32, (PAGE, hq), 0)
        a, carry = attend(_dot_nt(k_new, qbd), jnp.ones((SUBLANES, hq), F32), krow < (lane % seq))
        acc_scr[...] = _dot(a, v_new)
        carry_scr[...] = carry

    for i in range(PAGES_PER_STEP):
        kt16_scr[:, i * PAGE:(i + 1) * PAGE] = kpages[i][...].astype(BF16)
    z_all = _dot(qbd_scr[...], kt16_scr[...])
    carry = carry_scr[...]
    acc = acc_scr[...]
    for i in range(PAGES_PER_STEP):
        a, carry = attend(z_all[:, i * PAGE:(i + 1) * PAGE].T, carry)
        acc = acc + _dot_nt(a, vpages[i][...].astype(BF16))
    acc_scr[...] = acc
    carry_scr[...] = carry

    @pl.when(step == n_steps - 1)
    def _():
        o3 = acc.reshape(SB_HEADS, seq, SB_W)
        hsel = lax.broadcasted_iota(jnp.int32, (SB_HEADS, seq, SB_W), 0)
        lsel = lax.broadcasted_iota(jnp.int32, (SB_HEADS, seq, SB_W), 2) // SB_DH
        o_ref[...] = jnp.sum(jnp.where(hsel == lsel, o3, 0.0), axis=0)


def _sb_sample(q, k_new, v_new, cache_kt, cache_vt, page_table, bias):
    bz, seq, _ = q.shape
    n_pages = page_table.shape[1]
    assert n_pages % PAGES_PER_STEP == 0 and seq == SUBLANES
    n_steps = n_pages // PAGES_PER_STEP
    bias_lane = jnp.repeat(bias.astype(F32) * LOG2E, seq).reshape(1, SB_HEADS * seq)
    tok_spec = pl.BlockSpec((None, seq, SB_W), lambda b, s, pt: (b, 0, 0))

    def page_spec(i):
        def index(b, s, pt):
            return (pt[b * n_pages + n_pages - 1 - (s * PAGES_PER_STEP + i)], 0, 0)
        return pl.BlockSpec((None, SB_W, PAGE), index)

    pages = [page_spec(i) for i in range(PAGES_PER_STEP)]
    hq = SB_HEADS * seq
    return pl.pallas_call(
        functools.partial(_sb_sample_kernel, seq=seq, n_steps=n_steps),
        grid_spec=pltpu.PrefetchScalarGridSpec(
            num_scalar_prefetch=1,
            grid=(bz, n_steps),
            in_specs=[tok_spec, tok_spec, tok_spec, pl.BlockSpec((1, hq), lambda b, s, pt: (0, 0))] + pages + pages,
            out_specs=tok_spec,
            scratch_shapes=[pltpu.VMEM((hq, SB_W), BF16), pltpu.VMEM((SUBLANES, hq), F32),
                            pltpu.VMEM((hq, SB_W), F32), pltpu.VMEM((SB_W, PAGES_PER_STEP * PAGE), BF16)],
        ),
        out_shape=jax.ShapeDtypeStruct((bz, seq, SB_W), F32),
        compiler_params=pltpu.CompilerParams(dimension_semantics=("arbitrary", "arbitrary"),
                                             vmem_limit_bytes=VMEM_LIMIT),
        name="sb_sample",
    )(page_table.reshape(-1), q, k_new, v_new, bias_lane,
      *([cache_kt] * PAGES_PER_STEP), *([cache_vt] * PAGES_PER_STEP))


def _sb_out_kernel(o_ref, g_ref, x_ref, w_ref, y_ref):
    gated = (o_ref[...] * _silu(g_ref[...])).astype(BF16)
    y_ref[...] = x_ref[...] + _dot(gated, w_ref[...])


def _sb_out(o, g, x, w_out, *, tm):
    n = o.shape[0]
    spec = pl.BlockSpec((tm, D_MODEL), lambda i: (i, 0))
    return pl.pallas_call(
        _sb_out_kernel,
        grid=(n // tm,),
        in_specs=[spec, spec, spec, _const_spec((SB_W, D_MODEL))],
        out_specs=spec,
        out_shape=jax.ShapeDtypeStruct((n, D_MODEL), F32),
        compiler_params=pltpu.CompilerParams(dimension_semantics=("arbitrary",), vmem_limit_bytes=VMEM_LIMIT),
        name="sb_out_sample",
    )(o, g, x, w_out.astype(BF16))


def kernel(x_prompt, x_sample, state_gla, cache_k, cache_v, page_table, norm_gain, w_in_a, w_alpha_up, b_alpha, onorm_a, w_out_a, w_in_b, qnorm_b, knorm_b, sb_bias, w_out_b):
    bz, seq_p, _ = x_prompt.shape
    dec_b, dec_s, _ = x_sample.shape

    gla = functools.partial(_gla_layer, gain=norm_gain[0], w_in=w_in_a[0], w_up=w_alpha_up[0], b_a=b_alpha[0],
                            onorm=onorm_a[0], w_out=w_out_a[0])
    xp, st_p = gla(x_prompt, None, nb=1, tm=4 * GLA_CHUNK, chunk=GLA_CHUNK)
    xs, st_s = gla(x_sample, state_gla[0], nb=GLA_GROUP // dec_s, tm=dec_s, chunk=dec_s)

    k_p, v_p, qt, kp, vtp, gt = _sb_in(xp, norm_gain[1], w_in_b[0], qnorm_b[0], knorm_b[0], tm=TQ, prompt=True)
    ot = _sb_prompt(qt, kp, vtp, sb_bias[0])
    yp = _sb_out_t(ot, gt, xp, w_out_b[0])

    xs_flat = xs.reshape(1, dec_b * dec_s, D_MODEL)
    k_s, v_s, q_s, g_s = _sb_in(xs_flat, norm_gain[1], w_in_b[0], qnorm_b[0], knorm_b[0], tm=TQ, prompt=False)
    per_seq = (dec_b, dec_s, SB_W)
    n_phys = cache_k.shape[1]
    cache_kt = jnp.transpose(cache_k[0], (0, 2, 3, 1)).reshape(n_phys, SB_W, PAGE)
    cache_vt = jnp.transpose(cache_v[0], (0, 2, 3, 1)).reshape(n_phys, SB_W, PAGE)
    o_s = _sb_sample(q_s.reshape(per_seq), k_s.reshape(per_seq), v_s.reshape(per_seq),
                     cache_kt, cache_vt, page_table, sb_bias[0])
    ys = _sb_out(o_s.reshape(dec_b * dec_s, SB_W), g_s[0], xs_flat[0], w_out_b[0], tm=TQ)

    heads = (SB_HEADS, SB_DH)

    def from_feature_major(t):
        return jnp.transpose(t.reshape(bz, *heads, seq_p), (0, 3, 1, 2))[None]

    return (yp, ys.reshape(dec_b, dec_s, D_MODEL), st_p[None], st_s[None],
            from_feature_major(k_p), from_feature_major(v_p),
            k_s.reshape(1, dec_b, dec_s, *heads), v_s.reshape(1, dec_b, dec_s, *heads))
```

```python
import functools
import math

import jax
import jax.numpy as jnp
import numpy as np
from jax import lax
from jax.experimental import pallas as pl
from jax.experimental.pallas import tpu as pltpu

EPS = 1e-6
LOG2E = math.log2(math.e)

D_MODEL = 1024
GLA_HEADS = 4
GLA_DK = 128
GLA_DV = 256
GLA_KEY_W = GLA_HEADS * GLA_DK
GLA_VAL_W = GLA_HEADS * GLA_DV
GLA_RANK = 16
GLA_TAU = 16.0
GLA_CHUNK = 64
GLA_GROUP = 64
SB_HEADS = 16
SB_DH = 64
SB_W = SB_HEADS * SB_DH
PAGE = 128
LANES = 128
SUBLANES = 8
VMEM_LIMIT = 56 * 1024 * 1024

TQ = 256
TK = 128
QB = 512
GK = 512
Z2_MAX = 80.0

BF16 = jnp.bfloat16
F32 = jnp.float32


def _dot(a, b):
    return jnp.dot(a, b, preferred_element_type=F32)


def _dot_nt(a, b):
    return lax.dot_general(a, b, (((1,), (1,)), ((), ())), preferred_element_type=F32)


def _split2(x):
    hi = x.astype(BF16)
    lo = (x - hi.astype(F32)).astype(BF16)
    return hi, lo


def _log_sigmoid(x):
    return jnp.minimum(x, 0.0) - jnp.log(1.0 + jnp.exp(-jnp.abs(x)))


def _silu(x):
    return x / (1.0 + jnp.exp(-x))


def _rms_rows(x, gain):
    ms = jnp.mean(x * x, axis=-1, keepdims=True)
    return x * lax.rsqrt(ms + EPS) * gain


def _const_spec(shape):
    zeros = (0,) * len(shape)
    return pl.BlockSpec(shape, lambda *_: zeros, pipeline_mode=pl.Buffered(1))


def _gla_kernel(*refs, nb, tm, chunk, has_s0):
    if has_s0:
        (x_ref, s0_ref, gain_ref, wm_ref, wkt_ref, wac_ref, wact_ref, wup_ref, wupt_ref, ba_ref, bat_ref,
         on_ref, wout_ref, lcum_ref, mlater_ref, y_ref, sout_ref, s_scr, o_scr) = refs
    else:
        (x_ref, gain_ref, wm_ref, wkt_ref, wac_ref, wact_ref, wup_ref, wupt_ref, ba_ref, bat_ref,
         on_ref, wout_ref, lcum_ref, mlater_ref, y_ref, sout_ref, s_scr, o_scr) = refs
        s0_ref = None
    m = nb * tm
    step = pl.program_id(1)

    @pl.when(step == 0)
    def _():
        if has_s0:
            s_scr[...] = s0_ref[...]
        else:
            s_scr[...] = jnp.zeros_like(s_scr)

    x = x_ref[...].reshape(m, D_MODEL)
    h = _rms_rows(x, gain_ref[...]).astype(BF16)
    u = _dot(h, wm_ref[...])
    q = u[:, :GLA_KEY_W] * (GLA_DK ** -0.5)
    k = u[:, GLA_KEY_W:2 * GLA_KEY_W]
    v16 = u[:, 2 * GLA_KEY_W:2 * GLA_KEY_W + GLA_VAL_W].astype(BF16)
    gate = u[:, 2 * GLA_KEY_W + GLA_VAL_W:]

    code = _dot(h, wac_ref[...]).astype(BF16)
    la = _log_sigmoid(_dot(code, wup_ref[...]) + ba_ref[...]) * (1.0 / GLA_TAU)
    code_t = _dot_nt(wact_ref[...], h).astype(BF16)
    la_t = _log_sigmoid(_dot(wupt_ref[...], code_t) + bat_ref[...]) * (1.0 / GLA_TAU)
    k_t = k.T if m % LANES == 0 else _dot_nt(wkt_ref[...], h)

    la_hi, la_lo = _split2(la)
    b = _dot(lcum_ref[...], la_hi) + _dot(lcum_ref[...], la_lo)
    qd = (q * jnp.exp(b)).astype(BF16)
    kd = (k * jnp.exp(-b)).astype(BF16)
    lat_hi, lat_lo = _split2(la_t)
    rest_t = _dot(lat_hi, mlater_ref[...]) + _dot(lat_lo, mlater_ref[...])
    kl_t = k_t * jnp.exp(rest_t)

    lb = min(LANES, m)
    n_groups = m // GLA_GROUP
    n_sub = GLA_GROUP // chunk
    ri = lax.broadcasted_iota(jnp.int32, (GLA_GROUP, GLA_GROUP), 0)
    ci = lax.broadcasted_iota(jnp.int32, (GLA_GROUP, GLA_GROUP), 1)
    att_mask = ((ri // chunk) == (ci // chunk)) & (ci <= ri)

    for g in range(n_groups):
        r0 = g * GLA_GROUP
        rows = slice(r0, r0 + GLA_GROUP)
        blk = (r0 // lb) * lb
        cols = slice(blk, blk + lb)
        for hd in range(GLA_HEADS):
            fk = slice(hd * GLA_DK, (hd + 1) * GLA_DK)
            fv = slice(hd * GLA_DV, (hd + 1) * GLA_DV)
            qd_g = qd[rows, fk]
            att = jnp.where(att_mask, _dot_nt(qd_g, kd[rows, fk]), 0.0).astype(BF16)
            o = _dot(att, v16[rows, fv])
            for j in range(n_sub):
                sub = (r0 + j * chunk) // chunk
                si = (r0 + j * chunk) // tm
                s_old = s_scr[si, hd]
                o_st = _dot(qd_g, s_old.astype(BF16))
                if n_sub > 1:
                    rsel = lax.broadcasted_iota(jnp.int32, (GLA_GROUP, GLA_DV), 0)
                    o_st = jnp.where((rsel + r0) // chunk == sub, o_st, 0.0)
                o = o + o_st
                csel = (lax.broadcasted_iota(jnp.int32, (GLA_DK, lb), 1) + blk) // chunk == sub
                la_sel = jnp.where(csel, la_t[fk, cols], 0.0)
                kl_sel = jnp.where(csel, kl_t[fk, cols], 0.0).astype(BF16)
                decay = jnp.exp(jnp.sum(la_sel, axis=1, keepdims=True))
                s_scr[si, hd] = s_old * decay + _dot(kl_sel, v16[cols, fv])
            o_scr[rows, fv] = o

    y = x
    for hd in range(GLA_HEADS):
        fv = slice(hd * GLA_DV, (hd + 1) * GLA_DV)
        on = _rms_rows(o_scr[:, fv], on_ref[...])
        gated = (on * _silu(gate[:, fv])).astype(BF16)
        y = y + _dot(gated, wout_ref[fv, :])
    y_ref[...] = y.reshape(nb, tm, D_MODEL)

    @pl.when(step == pl.num_programs(1) - 1)
    def _():
        sout_ref[...] = s_scr[...]


def _block_structure(m, chunk):
    idx = np.arange(m)
    same = (idx[:, None] // chunk) == (idx[None, :] // chunk)
    lcum = same & (idx[None, :] <= idx[:, None])
    later = same & (idx[:, None] > idx[None, :])
    return jnp.asarray(lcum, BF16), jnp.asarray(later, BF16)


def _gla_layer(x, s0, gain, w_in, w_up, b_a, onorm, w_out, *, nb, tm, chunk):
    bz, length, _ = x.shape
    has_s0 = s0 is not None
    m = nb * tm
    assert bz % nb == 0 and length % tm == 0 and m % GLA_GROUP == 0
    assert chunk == GLA_GROUP or (chunk == tm and m == GLA_GROUP)
    wm = w_in[:, :2 * GLA_KEY_W + 2 * GLA_VAL_W].astype(BF16)
    wkt = w_in[:, GLA_KEY_W:2 * GLA_KEY_W].T.astype(BF16)
    wac = jnp.pad(w_in[:, 2 * GLA_KEY_W + 2 * GLA_VAL_W:], ((0, 0), (0, LANES - GLA_RANK))).astype(BF16)
    wup = jnp.pad(w_up, ((0, LANES - GLA_RANK), (0, 0))).astype(BF16)
    lcum, later = _block_structure(m, chunk)
    consts = [gain.reshape(1, D_MODEL), wm, wkt, wac, wac.T, wup, wup.T, b_a.reshape(1, GLA_KEY_W),
              b_a.reshape(GLA_KEY_W, 1), onorm.reshape(1, GLA_DV), w_out.astype(BF16), lcum, later]
    state_shape = (nb, GLA_HEADS, GLA_DK, GLA_DV)
    state_spec = pl.BlockSpec(state_shape, lambda b, l: (b, 0, 0, 0))
    x_spec = pl.BlockSpec((nb, tm, D_MODEL), lambda b, l: (b, l, 0))
    in_specs = [x_spec] + ([state_spec] if has_s0 else []) + [_const_spec(c.shape) for c in consts]
    args = [x] + ([s0] if has_s0 else []) + consts
    return pl.pallas_call(
        functools.partial(_gla_kernel, nb=nb, tm=tm, chunk=chunk, has_s0=has_s0),
        grid=(bz // nb, length // tm),
        in_specs=in_specs,
        out_specs=[x_spec, state_spec],
        out_shape=[jax.ShapeDtypeStruct(x.shape, F32),
                   jax.ShapeDtypeStruct((bz, GLA_HEADS, GLA_DK, GLA_DV), F32)],
        scratch_shapes=[pltpu.VMEM(state_shape, F32), pltpu.VMEM((m, GLA_VAL_W), F32)],
        compiler_params=pltpu.CompilerParams(dimension_semantics=("arbitrary", "arbitrary"),
                                             vmem_limit_bytes=VMEM_LIMIT),
        name="gla_layer_s0" if has_s0 else "gla_layer",
    )(*args)


def _key_permutation():
    p = np.arange(TK)
    return (TK // SUBLANES) * (p % SUBLANES) + p // SUBLANES


def _head_norm_t(t, gain_col, tokens):
    t3 = t.reshape(SB_HEADS, SB_DH, tokens)
    ms = jnp.mean(t3 * t3, axis=1, keepdims=True)
    return (t3 * lax.rsqrt(ms + EPS)).reshape(SB_W, tokens) * gain_col


def _sb_in_kernel(*refs, tm, prompt):
    if prompt:
        (x_ref, gain_ref, wt_ref, qg_ref, kg_ref, p_ref, pt_ref,
         k_ref, v_ref, qt_ref, kp_ref, vtp_ref, gt_ref) = refs
    else:
        (x_ref, gain_ref, wt_ref, qg_ref, kg_ref, k_ref, v_ref, q_ref, g_ref) = refs
    x = x_ref[0]
    h = _rms_rows(x, gain_ref[...]).astype(BF16)
    u_t = _dot_nt(wt_ref[...], h)
    q_t = _head_norm_t(u_t[:SB_W], qg_ref[...], tm) * (SB_DH ** -0.5 * (0.5 if prompt else LOG2E))
    k_t = _head_norm_t(u_t[SB_W:2 * SB_W], kg_ref[...], tm)
    v_t = u_t[2 * SB_W:3 * SB_W]
    g_t = u_t[3 * SB_W:]
    if prompt:
        k_ref[0] = k_t
        v_ref[0] = v_t
        qt_ref[...] = q_t.astype(BF16).reshape(SB_HEADS, SB_DH, tm)
        gt_ref[...] = g_t.astype(BF16)
        kp_ref[...] = _dot_nt(p_ref[...], k_t.astype(BF16)).astype(BF16).reshape(tm // TK, TK, SB_W)
        vp_t = _dot(v_t.astype(BF16), pt_ref[...])
        for j in range(tm // TK):
            vtp_ref[:, j] = vp_t[:, j * TK:(j + 1) * TK].astype(BF16).reshape(SB_HEADS, SB_DH, TK)
    else:
        k_ref[0] = k_t.T
        v_ref[0] = v_t.T
        q_ref[0] = q_t.T
        g_ref[0] = g_t.T


def _sb_in(x, gain, w_in, qn, kn, *, tm, prompt):
    bz, length, _ = x.shape
    assert length % tm == 0 and tm % TK == 0
    nt = length // tm
    wt = w_in.T.astype(BF16)
    qg = jnp.tile(qn, SB_HEADS).reshape(SB_W, 1)
    kg = jnp.tile(kn, SB_HEADS).reshape(SB_W, 1)
    consts = [gain.reshape(1, D_MODEL), wt, qg, kg]
    nat_spec = pl.BlockSpec((1, tm, SB_W), lambda b, l: (b, l, 0))
    nat_shape = jax.ShapeDtypeStruct((bz, length, SB_W), F32)
    if prompt:
        assert tm == TQ
        perm = np.concatenate([_key_permutation() + j * TK for j in range(tm // TK)])
        pt = np.zeros((tm, tm), np.float32)
        pt[perm, np.arange(tm)] = 1.0
        consts += [jnp.asarray(pt.T, BF16), jnp.asarray(pt, BF16)]
        t_spec = pl.BlockSpec((1, SB_W, tm), lambda b, l: (b, 0, l))
        t_shape = jax.ShapeDtypeStruct((bz, SB_W, length), F32)
        out_specs = [
            t_spec, t_spec,
            pl.BlockSpec((None, SB_HEADS, None, SB_DH, tm), lambda b, l: (b, 0, l, 0, 0)),
            pl.BlockSpec((None, tm // TK, TK, SB_W), lambda b, l: (b, l, 0, 0)),
            pl.BlockSpec((None, SB_HEADS, tm // TK, SB_DH, TK), lambda b, l: (b, 0, l, 0, 0)),
            pl.BlockSpec((None, None, SB_W, tm), lambda b, l: (b, l, 0, 0)),
        ]
        out_shape = [
            t_shape, t_shape,
            jax.ShapeDtypeStruct((bz, SB_HEADS, nt, SB_DH, tm), BF16),
            jax.ShapeDtypeStruct((bz, length // TK, TK, SB_W), BF16),
            jax.ShapeDtypeStruct((bz, SB_HEADS, length // TK, SB_DH, TK), BF16),
            jax.ShapeDtypeStruct((bz, nt, SB_W, tm), BF16),
        ]
    else:
        out_specs = [nat_spec] * 4
        out_shape = [nat_shape] * 4
    return pl.pallas_call(
        functools.partial(_sb_in_kernel, tm=tm, prompt=prompt),
        grid=(bz, nt),
        in_specs=[nat_spec] + [_const_spec(c.shape) for c in consts],
        out_specs=out_specs,
        out_shape=out_shape,
        compiler_params=pltpu.CompilerParams(dimension_semantics=("arbitrary", "arbitrary"),
                                             vmem_limit_bytes=VMEM_LIMIT),
        name="sb_in_prompt" if prompt else "sb_in_sample",
    )(x, *consts)


def _shift_rows_up(x, s, rid):
    return jnp.where(rid < SUBLANES - s, pltpu.roll(x, SUBLANES - s, axis=0), 1.0)


def _sb_tile(u, carry, rid, mask=None):
    n = u.shape[-1]
    rho = 0.5 - 0.5 * jnp.tanh(u)
    if mask is not None:
        rho = jnp.where(mask, rho, 1.0)
    nv = TK // SUBLANES
    r3 = rho.reshape(nv, SUBLANES, n)
    run = [None] * nv
    run[nv - 1] = r3[nv - 1]
    for i in range(nv - 2, -1, -1):
        run[i] = r3[i] * run[i + 1]
    tot = run[0]
    later = _shift_rows_up(tot, 1, rid)
    later = later * _shift_rows_up(later, 1, rid)
    later = later * _shift_rows_up(later, 2, rid)
    later = later * _shift_rows_up(later, 4, rid)
    grp = later * carry
    run.append(jnp.ones_like(tot))
    diff = jnp.concatenate([run[i + 1] - run[i] for i in range(nv)], axis=0).astype(BF16)
    grp2 = jnp.concatenate([grp, grp], axis=0).astype(BF16)
    a = (diff.reshape(nv // 2, 2 * SUBLANES, n) * grp2[None]).reshape(TK, n)
    new_carry = jnp.broadcast_to((grp * tot)[0:1], (SUBLANES, n))
    return a, new_carry


def _sb_prompt_kernel(qt_ref, kp_ref, vtp_ref, bias_ref, ones_ref, ot_ref, u_scr, a_scr, *, nq):
    rid = lax.broadcasted_iota(jnp.int32, (SUBLANES, LANES), 0)
    prow = lax.broadcasted_iota(jnp.int32, (TK, LANES), 0)
    key_in_tile = (TK // SUBLANES) * (prow % SUBLANES) + prow // SUBLANES
    qlane = lax.broadcasted_iota(jnp.int32, (TK, LANES), 1)
    zeros_q = jnp.zeros((SB_DH, QB), BF16)
    tiles = GK // TK
    halves = QB // TQ
    n_groups = nq * (nq + 1) // 2

    heads = range(2)

    def scores(h2, qb, g, slot):
        q = jnp.concatenate([qt_ref[h2, halves * qb + i] for i in range(halves)], axis=1)
        parts = [q, zeros_q] if h2 == 0 else [zeros_q, q]
        qaug = jnp.concatenate(parts + [bias_ref[h2]], axis=0)
        keys = kp_ref[pl.ds(g * tiles, tiles)].reshape(GK, LANES)
        u_scr[slot, h2] = _dot(jnp.concatenate([keys, ones_ref[...]], axis=1), qaug)

    def weights(h2, slot, carry, diag):
        out = []
        for c in range(QB // LANES):
            cs = slice(c * LANES, (c + 1) * LANES)
            car = carry[:, cs]
            for t in range(tiles - 1, -1, -1):
                rows = slice(t * TK, (t + 1) * TK)
                lead = t * TK - c * LANES
                if diag and lead >= LANES:
                    a_scr[slot, h2, rows, cs] = jnp.zeros((TK, LANES), BF16)
                    continue
                mask = (key_in_tile + lead) < qlane if diag and lead > -TK else None
                a, car = _sb_tile(u_scr[slot, h2, rows, cs], car, rid, mask)
                a_scr[slot, h2, rows, cs] = a
            out.append(car)
        return jnp.concatenate(out, axis=1)

    def values(h2, qb, g, slot):
        vt = jnp.concatenate([vtp_ref[h2, g * tiles + t] for t in range(tiles)], axis=1)
        part = _dot(vt, a_scr[slot, h2])
        for i in range(halves):
            prev = jnp.where(g == qb, 0.0, ot_ref[h2, halves * qb + i])
            ot_ref[h2, halves * qb + i] = prev + part[:, i * TQ:(i + 1) * TQ]

    def step(qb, g, pqb, pg, carries, *, slot):
        last = g == 0
        nqb = jnp.minimum(jnp.where(last, qb + 1, qb), nq - 1)
        ng = jnp.where(last, nqb, g - 1)

        def body(diag):
            def run(carries):
                ones = jnp.ones((SUBLANES, QB), F32)
                carries = tuple(weights(h2, slot, ones if diag else carries[h2], diag) for h2 in heads)
                for h2 in heads:
                    values(h2, pqb, pg, 1 - slot)
                    scores(h2, nqb, ng, 1 - slot)
                return carries
            return run

        return nqb, ng, qb, g, lax.cond(g == qb, body(True), body(False), carries)

    zero = jnp.zeros((), jnp.int32)
    for h2 in heads:
        scores(h2, zero, zero, 0)
    a_scr[1] = jnp.zeros((2, GK, QB), BF16)
    ot_ref[...] = jnp.zeros_like(ot_ref)
    st = (zero, zero, zero, zero, (jnp.ones((SUBLANES, QB), F32),) * 2)
    first = n_groups % 2
    if first:
        st = step(*st, slot=0)
    st = lax.fori_loop(0, n_groups // 2, lambda _, s: step(*step(*s, slot=first), slot=1 - first), st)
    for h2 in heads:
        values(h2, st[2], st[3], (n_groups - 1) % 2)


def _sb_prompt(qt, kp, vtp, bias):
    bz, _, nq_t, _, _ = qt.shape
    nk = kp.shape[1]
    assert (nq_t * TQ) % QB == 0
    nq = nq_t * TQ // QB
    b2 = bias.astype(F32) * 0.5
    hi = b2.astype(BF16)
    mid = (b2 - hi.astype(F32)).astype(BF16)
    lo = (b2 - hi.astype(F32) - mid.astype(F32)).astype(BF16)
    rows = jnp.stack([hi, mid, lo], axis=1)
    bias_rows = jnp.zeros((SB_HEADS, LANES, QB), BF16).at[:, :3, :].set(
        jnp.broadcast_to(rows[:, :, None], (SB_HEADS, 3, QB)))
    bias_rows = bias_rows.reshape(SB_HEADS // 2, 2, LANES, QB)
    ones_cols = jnp.zeros((GK, LANES), BF16).at[:, :3].set(1.0)
    pairs = SB_HEADS // 2
    return pl.pallas_call(
        functools.partial(_sb_prompt_kernel, nq=nq),
        grid=(bz * pairs,),
        in_specs=[
            pl.BlockSpec((None, 2, nq_t, SB_DH, TQ), lambda g: (g // pairs, g % pairs, 0, 0, 0)),
            pl.BlockSpec((None, nk, TK, LANES), lambda g: (g // pairs, 0, 0, g % pairs)),
            pl.BlockSpec((None, 2, nk, SB_DH, TK), lambda g: (g // pairs, g % pairs, 0, 0, 0)),
            pl.BlockSpec((None, 2, LANES, QB), lambda g: (g % pairs, 0, 0, 0)),
            _const_spec((GK, LANES)),
        ],
        out_specs=pl.BlockSpec((None, 2, nq_t, SB_DH, TQ), lambda g: (g // pairs, g % pairs, 0, 0, 0)),
        out_shape=jax.ShapeDtypeStruct((bz, SB_HEADS, nq_t, SB_DH, TQ), F32),
        scratch_shapes=[pltpu.VMEM((2, 2, GK, QB), F32), pltpu.VMEM((2, 2, GK, QB), BF16)],
        compiler_params=pltpu.CompilerParams(dimension_semantics=("arbitrary",),
                                             vmem_limit_bytes=VMEM_LIMIT),
        name="sb_prompt",
    )(qt, kp, vtp, bias_rows, ones_cols)


def _sb_out_t_kernel(ot_ref, gt_ref, x_ref, w_ref, y_ref):
    o_t = ot_ref[...].reshape(SB_W, TQ)
    gated = (o_t * _silu(gt_ref[...].astype(F32))).T.astype(BF16)
    y_ref[0] = x_ref[0] + _dot(gated, w_ref[...])


def _sb_out_t(ot, gt, x, w_out):
    bz, _, nq, _, _ = ot.shape
    x_spec = pl.BlockSpec((1, TQ, D_MODEL), lambda b, l: (b, l, 0))
    return pl.pallas_call(
        _sb_out_t_kernel,
        grid=(bz, nq),
        in_specs=[
            pl.BlockSpec((None, SB_HEADS, None, SB_DH, TQ), lambda b, l: (b, 0, l, 0, 0)),
            pl.BlockSpec((None, None, SB_W, TQ), lambda b, l: (b, l, 0, 0)),
            x_spec,
            _const_spec((SB_W, D_MODEL)),
        ],
        out_specs=x_spec,
        out_shape=jax.ShapeDtypeStruct(x.shape, F32),
        compiler_params=pltpu.CompilerParams(dimension_semantics=("arbitrary", "arbitrary"),
                                             vmem_limit_bytes=VMEM_LIMIT),
        name="sb_out_prompt",
    )(ot, gt, x, w_out.astype(BF16))


PAGES_PER_STEP = 16


def _rev_cumprod_rows(m, carry):
    rows, n = m.shape
    rid = lax.broadcasted_iota(jnp.int32, (rows, n), 0) % SUBLANES
    p = m
    for s in (1, 2, 4):
        p = p * jnp.where(rid < SUBLANES - s, pltpu.roll(p, rows - s, axis=0), 1.0)
    nv = rows // SUBLANES
    p3 = p.reshape(nv, SUBLANES, n)
    out = [None] * nv
    for i in range(nv - 1, -1, -1):
        out[i] = p3[i] * carry
        carry = jnp.broadcast_to(out[i][0:1], (SUBLANES, n))
    return jnp.concatenate(out, axis=0), carry


def _sb_sample_kernel(pt_ref, q_ref, kn_ref, vn_ref, bias_ref, *refs, seq, n_steps):
    kpages = refs[:PAGES_PER_STEP]
    vpages = refs[PAGES_PER_STEP:2 * PAGES_PER_STEP]
    o_ref = refs[2 * PAGES_PER_STEP]
    qbd_scr, carry_scr, acc_scr, kt16_scr = refs[2 * PAGES_PER_STEP + 1:]
    step = pl.program_id(1)
    hq = SB_HEADS * seq

    def attend(u, carry, mask=None):
        w = jnp.exp2(jnp.minimum(u + bias_ref[...], Z2_MAX))
        if mask is not None:
            w = jnp.where(mask, w, 0.0)
        prod, carry = _rev_cumprod_rows(1.0 + w, carry)
        return (w / prod).T.astype(BF16), carry

    @pl.when(step == 0)
    def _():
        q = q_ref[...]
        qrow = jnp.concatenate([q] * SB_HEADS, axis=0)
        rh = lax.broadcasted_iota(jnp.int32, (hq, SB_W), 0) // seq
        lh = lax.broadcasted_iota(jnp.int32, (hq, SB_W), 1) // SB_DH
        qbd = jnp.where(rh == lh, qrow, 0.0).astype(BF16)
        qbd_scr[...] = qbd
        pad = jnp.zeros((PAGE - seq, SB_W), F32)
        k_new = jnp.concatenate([kn_ref[...], pad], axis=0).astype(BF16)
        v_new = jnp.concatenate([vn_ref[...], pad], axis=0).astype(BF16)
        lane = lax.broadcasted_iota(jnp.int32, (PAGE, hq), 1)
        krow = lax.broadcasted_iota(jnp.int32, (PAGE, hq), 0)
        a, carry = attend(_dot_nt(k_new, qbd), jnp.ones((SUBLANES, hq), F32), krow < (lane % seq))
        acc_scr[...] = _dot(a, v_new)
        carry_scr[...] = carry

    for i in range(PAGES_PER_STEP):
        kt16_scr[:, i * PAGE:(i + 1) * PAGE] = kpages[i][...].astype(BF16)
    z_all = _dot(qbd_scr[...], kt16_scr[...])
    carry = carry_scr[...]
    acc = acc_scr[...]
    for i in range(PAGES_PER_STEP):
        a, carry = attend(z_all[:, i * PAGE:(i + 1) * PAGE].T, carry)
        acc = acc + _dot_nt(a, vpages[i][...].astype(BF16))
    acc_scr[...] = acc
    carry_scr[...] = carry

    @pl.when(step == n_steps - 1)
    def _():
        o3 = acc.reshape(SB_HEADS, seq, SB_W)
        hsel = lax.broadcasted_iota(jnp.int32, (SB_HEADS, seq, SB_W), 0)
        lsel = lax.broadcasted_iota(jnp.int32, (SB_HEADS, seq, SB_W), 2) // SB_DH
        o_ref[...] = jnp.sum(jnp.where(hsel == lsel, o3, 0.0), axis=0)


def _sb_sample(q, k_new, v_new, cache_kt, cache_vt, page_table, bias):
    bz, seq, _ = q.shape
    n_pages = page_table.shape[1]
    assert n_pages % PAGES_PER_STEP == 0 and seq == SUBLANES
    n_steps = n_pages // PAGES_PER_STEP
    bias_lane = jnp.repeat(bias.astype(F32) * LOG2E, seq).reshape(1, SB_HEADS * seq)
    tok_spec = pl.BlockSpec((None, seq, SB_W), lambda b, s, pt: (b, 0, 0))

    def page_spec(i):
        def index(b, s, pt):
            return (pt[b * n_pages + n_pages - 1 - (s * PAGES_PER_STEP + i)], 0, 0)
        return pl.BlockSpec((None, SB_W, PAGE), index)

    pages = [page_spec(i) for i in range(PAGES_PER_STEP)]
    hq = SB_HEADS * seq
    return pl.pallas_call(
        functools.partial(_sb_sample_kernel, seq=seq, n_steps=n_steps),
        grid_spec=pltpu.PrefetchScalarGridSpec(
            num_scalar_prefetch=1,
            grid=(bz, n_steps),
            in_specs=[tok_spec, tok_spec, tok_spec, pl.BlockSpec((1, hq), lambda b, s, pt: (0, 0))] + pages + pages,
            out_specs=tok_spec,
            scratch_shapes=[pltpu.VMEM((hq, SB_W), BF16), pltpu.VMEM((SUBLANES, hq), F32),
                            pltpu.VMEM((hq, SB_W), F32), pltpu.VMEM((SB_W, PAGES_PER_STEP * PAGE), BF16)],
        ),
        out_shape=jax.ShapeDtypeStruct((bz, seq, SB_W), F32),
        compiler_params=pltpu.CompilerParams(dimension_semantics=("arbitrary", "arbitrary"),
                                             vmem_limit_bytes=VMEM_LIMIT),
        name="sb_sample",
    )(page_table.reshape(-1), q, k_new, v_new, bias_lane,
      *([cache_kt] * PAGES_PER_STEP), *([cache_vt] * PAGES_PER_STEP))


def _sb_out_kernel(o_ref, g_ref, x_ref, w_ref, y_ref):
    gated = (o_ref[...] * _silu(g_ref[...])).astype(BF16)
    y_ref[...] = x_ref[...] + _dot(gated, w_ref[...])


def _sb_out(o, g, x, w_out, *, tm):
    n = o.shape[0]
    spec = pl.BlockSpec((tm, D_MODEL), lambda i: (i, 0))
    return pl.pallas_call(
        _sb_out_kernel,
        grid=(n // tm,),
        in_specs=[spec, spec, spec, _const_spec((SB_W, D_MODEL))],
        out_specs=spec,
        out_shape=jax.ShapeDtypeStruct((n, D_MODEL), F32),
        compiler_params=pltpu.CompilerParams(dimension_semantics=("arbitrary",), vmem_limit_bytes=VMEM_LIMIT),
        name="sb_out_sample",
    )(o, g, x, w_out.astype(BF16))


def kernel(x_prompt, x_sample, state_gla, cache_k, cache_v, page_table, norm_gain, w_in_a, w_alpha_up, b_alpha, onorm_a, w_out_a, w_in_b, qnorm_b, knorm_b, sb_bias, w_out_b):
    bz, seq_p, _ = x_prompt.shape
    dec_b, dec_s, _ = x_sample.shape

    gla = functools.partial(_gla_layer, gain=norm_gain[0], w_in=w_in_a[0], w_up=w_alpha_up[0], b_a=b_alpha[0],
                            onorm=onorm_a[0], w_out=w_out_a[0])
    xp, st_p = gla(x_prompt, None, nb=1, tm=4 * GLA_CHUNK, chunk=GLA_CHUNK)
    xs, st_s = gla(x_sample, state_gla[0], nb=GLA_GROUP // dec_s, tm=dec_s, chunk=dec_s)

    k_p, v_p, qt, kp, vtp, gt = _sb_in(xp, norm_gain[1], w_in_b[0], qnorm_b[0], knorm_b[0], tm=TQ, prompt=True)
    ot = _sb_prompt(qt, kp, vtp, sb_bias[0])
    yp = _sb_out_t(ot, gt, xp, w_out_b[0])

    xs_flat = xs.reshape(1, dec_b * dec_s, D_MODEL)
    k_s, v_s, q_s, g_s = _sb_in(xs_flat, norm_gain[1], w_in_b[0], qnorm_b[0], knorm_b[0], tm=TQ, prompt=False)
    per_seq = (dec_b, dec_s, SB_W)
    n_phys = cache_k.shape[1]
    cache_kt = jnp.transpose(cache_k[0], (0, 2, 3, 1)).reshape(n_phys, SB_W, PAGE)
    cache_vt = jnp.transpose(cache_v[0], (0, 2, 3, 1)).reshape(n_phys, SB_W, PAGE)
    o_s = _sb_sample(q_s.reshape(per_seq), k_s.reshape(per_seq), v_s.reshape(per_seq),
                     cache_kt, cache_vt, page_table, sb_bias[0])
    ys = _sb_out(o_s.reshape(dec_b * dec_s, SB_W), g_s[0], xs_flat[0], w_out_b[0], tm=TQ)

    heads = (SB_HEADS, SB_DH)

    def from_feature_major(t):
        return jnp.transpose(t.reshape(bz, *heads, seq_p), (0, 3, 1, 2))[None]

    return (yp, ys.reshape(dec_b, dec_s, D_MODEL), st_p[None], st_s[None],
            from_feature_major(k_p), from_feature_major(v_p),
            k_s.reshape(1, dec_b, dec_s, *heads), v_s.reshape(1, dec_b, dec_s, *heads))
```

```python
import functools
import math

import jax
import jax.numpy as jnp
import numpy as np
from jax import lax
from jax.experimental import pallas as pl
from jax.experimental.pallas import tpu as pltpu

EPS = 1e-6
LOG2E = math.log2(math.e)

D_MODEL = 1024
GLA_HEADS = 4
GLA_DK = 128
GLA_DV = 256
GLA_KEY_W = GLA_HEADS * GLA_DK
GLA_VAL_W = GLA_HEADS * GLA_DV
GLA_RANK = 16
GLA_TAU = 16.0
GLA_CHUNK = 64
GLA_GROUP = 64
SB_HEADS = 16
SB_DH = 64
SB_W = SB_HEADS * SB_DH
PAGE = 128
LANES = 128
SUBLANES = 8
VMEM_LIMIT = 56 * 1024 * 1024

TQ = 256
TK = 128
QB = 512
GK = 512
Z2_MAX = 80.0

BF16 = jnp.bfloat16
F32 = jnp.float32


def _dot(a, b):
    return jnp.dot(a, b, preferred_element_type=F32)


def _dot_nt(a, b):
    return lax.dot_general(a, b, (((1,), (1,)), ((), ())), preferred_element_type=F32)


def _split2(x):
    hi = x.astype(BF16)
    lo = (x - hi.astype(F32)).astype(BF16)
    return hi, lo


def _log_sigmoid(x):
    return jnp.minimum(x, 0.0) - jnp.log(1.0 + jnp.exp(-jnp.abs(x)))


def _silu(x):
    return x / (1.0 + jnp.exp(-x))


def _rms_rows(x, gain):
    ms = jnp.mean(x * x, axis=-1, keepdims=True)
    return x * lax.rsqrt(ms + EPS) * gain


def _const_spec(shape):
    zeros = (0,) * len(shape)
    return pl.BlockSpec(shape, lambda *_: zeros, pipeline_mode=pl.Buffered(1))


def _gla_kernel(*refs, nb, tm, chunk, has_s0):
    if has_s0:
        (x_ref, s0_ref, gain_ref, wm_ref, wkt_ref, wac_ref, wact_ref, wup_ref, wupt_ref, ba_ref, bat_ref,
         on_ref, wout_ref, lcum_ref, mlater_ref, y_ref, sout_ref, s_scr, o_scr) = refs
    else:
        (x_ref, gain_ref, wm_ref, wkt_ref, wac_ref, wact_ref, wup_ref, wupt_ref, ba_ref, bat_ref,
         on_ref, wout_ref, lcum_ref, mlater_ref, y_ref, sout_ref, s_scr, o_scr) = refs
        s0_ref = None
    m = nb * tm
    step = pl.program_id(1)

    @pl.when(step == 0)
    def _():
        if has_s0:
            s_scr[...] = s0_ref[...]
        else:
            s_scr[...] = jnp.zeros_like(s_scr)

    x = x_ref[...].reshape(m, D_MODEL)
    h = _rms_rows(x, gain_ref[...]).astype(BF16)
    u = _dot(h, wm_ref[...])
    q = u[:, :GLA_KEY_W] * (GLA_DK ** -0.5)
    k = u[:, GLA_KEY_W:2 * GLA_KEY_W]
    v16 = u[:, 2 * GLA_KEY_W:2 * GLA_KEY_W + GLA_VAL_W].astype(BF16)
    gate = u[:, 2 * GLA_KEY_W + GLA_VAL_W:]

    code = _dot(h, wac_ref[...]).astype(BF16)
    la = _log_sigmoid(_dot(code, wup_ref[...]) + ba_ref[...]) * (1.0 / GLA_TAU)
    code_t = _dot_nt(wact_ref[...], h).astype(BF16)
    la_t = _log_sigmoid(_dot(wupt_ref[...], code_t) + bat_ref[...]) * (1.0 / GLA_TAU)
    k_t = k.T if m % LANES == 0 else _dot_nt(wkt_ref[...], h)

    la_hi, la_lo = _split2(la)
    b = _dot(lcum_ref[...], la_hi) + _dot(lcum_ref[...], la_lo)
    qd = (q * jnp.exp(b)).astype(BF16)
    kd = (k * jnp.exp(-b)).astype(BF16)
    lat_hi, lat_lo = _split2(la_t)
    rest_t = _dot(lat_hi, mlater_ref[...]) + _dot(lat_lo, mlater_ref[...])
    kl_t = k_t * jnp.exp(rest_t)

    lb = min(LANES, m)
    n_groups = m // GLA_GROUP
    n_sub = GLA_GROUP // chunk
    ri = lax.broadcasted_iota(jnp.int32, (GLA_GROUP, GLA_GROUP), 0)
    ci = lax.broadcasted_iota(jnp.int32, (GLA_GROUP, GLA_GROUP), 1)
    att_mask = ((ri // chunk) == (ci // chunk)) & (ci <= ri)

    for g in range(n_groups):
        r0 = g * GLA_GROUP
        rows = slice(r0, r0 + GLA_GROUP)
        blk = (r0 // lb) * lb
        cols = slice(blk, blk + lb)
        for hd in range(GLA_HEADS):
            fk = slice(hd * GLA_DK, (hd + 1) * GLA_DK)
            fv = slice(hd * GLA_DV, (hd + 1) * GLA_DV)
            qd_g = qd[rows, fk]
            att = jnp.where(att_mask, _dot_nt(qd_g, kd[rows, fk]), 0.0).astype(BF16)
            o = _dot(att, v16[rows, fv])
            for j in range(n_sub):
                sub = (r0 + j * chunk) // chunk
                si = (r0 + j * chunk) // tm
                s_old = s_scr[si, hd]
                o_st = _dot(qd_g, s_old.astype(BF16))
                if n_sub > 1:
                    rsel = lax.broadcasted_iota(jnp.int32, (GLA_GROUP, GLA_DV), 0)
                    o_st = jnp.where((rsel + r0) // chunk == sub, o_st, 0.0)
                o = o + o_st
                csel = (lax.broadcasted_iota(jnp.int32, (GLA_DK, lb), 1) + blk) // chunk == sub
                la_sel = jnp.where(csel, la_t[fk, cols], 0.0)
                kl_sel = jnp.where(csel, kl_t[fk, cols], 0.0).astype(BF16)
                decay = jnp.exp(jnp.sum(la_sel, axis=1, keepdims=True))
                s_scr[si, hd] = s_old * decay + _dot(kl_sel, v16[cols, fv])
            o_scr[rows, fv] = o

    y = x
    for hd in range(GLA_HEADS):
        fv = slice(hd * GLA_DV, (hd + 1) * GLA_DV)
        on = _rms_rows(o_scr[:, fv], on_ref[...])
        gated = (on * _silu(gate[:, fv])).astype(BF16)
        y = y + _dot(gated, wout_ref[fv, :])
    y_ref[...] = y.reshape(nb, tm, D_MODEL)

    @pl.when(step == pl.num_programs(1) - 1)
    def _():
        sout_ref[...] = s_scr[...]


def _block_structure(m, chunk):
    idx = np.arange(m)
    same = (idx[:, None] // chunk) == (idx[None, :] // chunk)
    lcum = same & (idx[None, :] <= idx[:, None])
    later = same & (idx[:, None] > idx[None, :])
    return jnp.asarray(lcum, BF16), jnp.asarray(later, BF16)


def _gla_layer(x, s0, gain, w_in, w_up, b_a, onorm, w_out, *, nb, tm, chunk):
    bz, length, _ = x.shape
    has_s0 = s0 is not None
    m = nb * tm
    assert bz % nb == 0 and length % tm == 0 and m % GLA_GROUP == 0
    assert chunk == GLA_GROUP or (chunk == tm and m == GLA_GROUP)
    wm = w_in[:, :2 * GLA_KEY_W + 2 * GLA_VAL_W].astype(BF16)
    wkt = w_in[:, GLA_KEY_W:2 * GLA_KEY_W].T.astype(BF16)
    wac = jnp.pad(w_in[:, 2 * GLA_KEY_W + 2 * GLA_VAL_W:], ((0, 0), (0, LANES - GLA_RANK))).astype(BF16)
    wup = jnp.pad(w_up, ((0, LANES - GLA_RANK), (0, 0))).astype(BF16)
    lcum, later = _block_structure(m, chunk)
    consts = [gain.reshape(1, D_MODEL), wm, wkt, wac, wac.T, wup, wup.T, b_a.reshape(1, GLA_KEY_W),
              b_a.reshape(GLA_KEY_W, 1), onorm.reshape(1, GLA_DV), w_out.astype(BF16), lcum, later]
    state_shape = (nb, GLA_HEADS, GLA_DK, GLA_DV)
    state_spec = pl.BlockSpec(state_shape, lambda b, l: (b, 0, 0, 0))
    x_spec = pl.BlockSpec((nb, tm, D_MODEL), lambda b, l: (b, l, 0))
    in_specs = [x_spec] + ([state_spec] if has_s0 else []) + [_const_spec(c.shape) for c in consts]
    args = [x] + ([s0] if has_s0 else []) + consts
    return pl.pallas_call(
        functools.partial(_gla_kernel, nb=nb, tm=tm, chunk=chunk, has_s0=has_s0),
        grid=(bz // nb, length // tm),
        in_specs=in_specs,
        out_specs=[x_spec, state_spec],
        out_shape=[jax.ShapeDtypeStruct(x.shape, F32),
                   jax.ShapeDtypeStruct((bz, GLA_HEADS, GLA_DK, GLA_DV), F32)],
        scratch_shapes=[pltpu.VMEM(state_shape, F32), pltpu.VMEM((m, GLA_VAL_W), F32)],
        compiler_params=pltpu.CompilerParams(dimension_semantics=("arbitrary", "arbitrary"),
                                             vmem_limit_bytes=VMEM_LIMIT),
        name="gla_layer_s0" if has_s0 else "gla_layer",
    )(*args)


def _key_permutation():
    p = np.arange(TK)
    return (TK // SUBLANES) * (p % SUBLANES) + p // SUBLANES


def _head_norm_t(t, gain_col, tokens):
    t3 = t.reshape(SB_HEADS, SB_DH, tokens)
    ms = jnp.mean(t3 * t3, axis=1, keepdims=True)
    return (t3 * lax.rsqrt(ms + EPS)).reshape(SB_W, tokens) * gain_col


def _sb_in_kernel(*refs, tm, prompt):
    if prompt:
        (x_ref, gain_ref, wt_ref, qg_ref, kg_ref, p_ref, pt_ref,
         k_ref, v_ref, qt_ref, kp_ref, vtp_ref, gt_ref) = refs
    else:
        (x_ref, gain_ref, wt_ref, qg_ref, kg_ref, k_ref, v_ref, q_ref, g_ref) = refs
    x = x_ref[0]
    h = _rms_rows(x, gain_ref[...]).astype(BF16)
    u_t = _dot_nt(wt_ref[...], h)
    q_t = _head_norm_t(u_t[:SB_W], qg_ref[...], tm) * (SB_DH ** -0.5 * (0.5 if prompt else LOG2E))
    k_t = _head_norm_t(u_t[SB_W:2 * SB_W], kg_ref[...], tm)
    v_t = u_t[2 * SB_W:3 * SB_W]
    g_t = u_t[3 * SB_W:]
    if prompt:
        k_ref[0] = k_t
        v_ref[0] = v_t
        qt_ref[...] = q_t.astype(BF16).reshape(SB_HEADS, SB_DH, tm)
        gt_ref[...] = g_t.astype(BF16)
        kp_ref[...] = _dot_nt(p_ref[...], k_t.astype(BF16)).astype(BF16).reshape(tm // TK, TK, SB_W)
        vp_t = _dot(v_t.astype(BF16), pt_ref[...])
        for j in range(tm // TK):
            vtp_ref[:, j] = vp_t[:, j * TK:(j + 1) * TK].astype(BF16).reshape(SB_HEADS, SB_DH, TK)
    else:
        k_ref[0] = k_t.T
        v_ref[0] = v_t.T
        q_ref[0] = q_t.T
        g_ref[0] = g_t.T


def _sb_in(x, gain, w_in, qn, kn, *, tm, prompt):
    bz, length, _ = x.shape
    assert length % tm == 0 and tm % TK == 0
    nt = length // tm
    wt = w_in.T.astype(BF16)
    qg = jnp.tile(qn, SB_HEADS).reshape(SB_W, 1)
    kg = jnp.tile(kn, SB_HEADS).reshape(SB_W, 1)
    consts = [gain.reshape(1, D_MODEL), wt, qg, kg]
    nat_spec = pl.BlockSpec((1, tm, SB_W), lambda b, l: (b, l, 0))
    nat_shape = jax.ShapeDtypeStruct((bz, length, SB_W), F32)
    if prompt:
        assert tm == TQ
        perm = np.concatenate([_key_permutation() + j * TK for j in range(tm // TK)])
        pt = np.zeros((tm, tm), np.float32)
        pt[perm, np.arange(tm)] = 1.0
        consts += [jnp.asarray(pt.T, BF16), jnp.asarray(pt, BF16)]
        t_spec = pl.BlockSpec((1, SB_W, tm), lambda b, l: (b, 0, l))
        t_shape = jax.ShapeDtypeStruct((bz, SB_W, length), F32)
        out_specs = [
            t_spec, t_spec,
            pl.BlockSpec((None, SB_HEADS, None, SB_DH, tm), lambda b, l: (b, 0, l, 0, 0)),
            pl.BlockSpec((None, tm // TK, TK, SB_W), lambda b, l: (b, l, 0, 0)),
            pl.BlockSpec((None, SB_HEADS, tm // TK, SB_DH, TK), lambda b, l: (b, 0, l, 0, 0)),
            pl.BlockSpec((None, None, SB_W, tm), lambda b, l: (b, l, 0, 0)),
        ]
        out_shape = [
            t_shape, t_shape,
            jax.ShapeDtypeStruct((bz, SB_HEADS, nt, SB_DH, tm), BF16),
            jax.ShapeDtypeStruct((bz, length // TK, TK, SB_W), BF16),
            jax.ShapeDtypeStruct((bz, SB_HEADS, length // TK, SB_DH, TK), BF16),
            jax.ShapeDtypeStruct((bz, nt, SB_W, tm), BF16),
        ]
    else:
        out_specs = [nat_spec] * 4
        out_shape = [nat_shape] * 4
    return pl.pallas_call(
        functools.partial(_sb_in_kernel, tm=tm, prompt=prompt),
        grid=(bz, nt),
        in_specs=[nat_spec] + [_const_spec(c.shape) for c in consts],
        out_specs=out_specs,
        out_shape=out_shape,
        compiler_params=pltpu.CompilerParams(dimension_semantics=("arbitrary", "arbitrary"),
                                             vmem_limit_bytes=VMEM_LIMIT),
        name="sb_in_prompt" if prompt else "sb_in_sample",
    )(x, *consts)


def _shift_rows_up(x, s, rid):
    return jnp.where(rid < SUBLANES - s, pltpu.roll(x, SUBLANES - s, axis=0), 1.0)


def _sb_tile(u, carry, rid, mask=None):
    n = u.shape[-1]
    rho = 0.5 - 0.5 * jnp.tanh(u)
    if mask is not None:
        rho = jnp.where(mask, rho, 1.0)
    nv = TK // SUBLANES
    r3 = rho.reshape(nv, SUBLANES, n)
    run = [None] * nv
    run[nv - 1] = r3[nv - 1]
    for i in range(nv - 2, -1, -1):
        run[i] = r3[i] * run[i + 1]
    tot = run[0]
    later = _shift_rows_up(tot, 1, rid)
    later = later * _shift_rows_up(later, 1, rid)
    later = later * _shift_rows_up(later, 2, rid)
    later = later * _shift_rows_up(later, 4, rid)
    grp = later * carry
    run.append(jnp.ones_like(tot))
    diff = jnp.concatenate([run[i + 1] - run[i] for i in range(nv)], axis=0).astype(BF16)
    grp2 = jnp.concatenate([grp, grp], axis=0).astype(BF16)
    a = (diff.reshape(nv // 2, 2 * SUBLANES, n) * grp2[None]).reshape(TK, n)
    new_carry = jnp.broadcast_to((grp * tot)[0:1], (SUBLANES, n))
    return a, new_carry


def _sb_prompt_kernel(qt_ref, kp_ref, vtp_ref, bias_ref, ones_ref, ot_ref, u_scr, a_scr, *, nq):
    rid = lax.broadcasted_iota(jnp.int32, (SUBLANES, LANES), 0)
    prow = lax.broadcasted_iota(jnp.int32, (TK, LANES), 0)
    key_in_tile = (TK // SUBLANES) * (prow % SUBLANES) + prow // SUBLANES
    qlane = lax.broadcasted_iota(jnp.int32, (TK, LANES), 1)
    zeros_q = jnp.zeros((SB_DH, QB), BF16)
    tiles = GK // TK
    halves = QB // TQ
    strips = QB // LANES
    n_groups = nq * (nq + 1) // 2

    heads = range(2)

    def scores(h2, qb, g, slot):
        q = jnp.concatenate([qt_ref[h2, halves * qb + i] for i in range(halves)], axis=1)
        parts = [q, zeros_q] if h2 == 0 else [zeros_q, q]
        qaug = jnp.concatenate(parts + [bias_ref[h2]], axis=0)
        keys = kp_ref[pl.ds(g * tiles, tiles)].reshape(GK, LANES)
        u = _dot(jnp.concatenate([keys, ones_ref[...]], axis=1), qaug)
        for c in range(strips):
            u_scr[slot, h2, c] = u[:, c * LANES:(c + 1) * LANES]

    def weights(h2, slot, carry, diag):
        out = []
        for c in range(strips):
            cs = slice(c * LANES, (c + 1) * LANES)
            car = carry[:, cs]
            for t in range(tiles - 1, -1, -1):
                rows = slice(t * TK, (t + 1) * TK)
                lead = t * TK - c * LANES
                if diag and lead >= LANES:
                    a_scr[slot, h2, c, rows] = jnp.zeros((TK, LANES), BF16)
                    continue
                mask = (key_in_tile + lead) < qlane if diag and lead > -TK else None
                a, car = _sb_tile(u_scr[slot, h2, c, rows], car, rid, mask)
                a_scr[slot, h2, c, rows] = a
            out.append(car)
        return jnp.concatenate(out, axis=1)

    def values(h2, qb, g, slot):
        vt = jnp.concatenate([vtp_ref[h2, g * tiles + t] for t in range(tiles)], axis=1)
        part = _dot(vt, jnp.concatenate([a_scr[slot, h2, c] for c in range(strips)], axis=1))
        for i in range(halves):
            prev = jnp.where(g == qb, 0.0, ot_ref[h2, halves * qb + i])
            ot_ref[h2, halves * qb + i] = prev + part[:, i * TQ:(i + 1) * TQ]

    def step(qb, g, pqb, pg, carries, *, slot):
        last = g == 0
        nqb = jnp.minimum(jnp.where(last, qb + 1, qb), nq - 1)
        ng = jnp.where(last, nqb, g - 1)

        def body(diag):
            def run(carries):
                ones = jnp.ones((SUBLANES, QB), F32)
                carries = tuple(weights(h2, slot, ones if diag else carries[h2], diag) for h2 in heads)
                for h2 in heads:
                    values(h2, pqb, pg, 1 - slot)
                    scores(h2, nqb, ng, 1 - slot)
                return carries
            return run

        return nqb, ng, qb, g, lax.cond(g == qb, body(True), body(False), carries)

    zero = jnp.zeros((), jnp.int32)
    for h2 in heads:
        scores(h2, zero, zero, 0)
    a_scr[1] = jnp.zeros((2, strips, GK, LANES), BF16)
    ot_ref[...] = jnp.zeros_like(ot_ref)
    st = (zero, zero, zero, zero, (jnp.ones((SUBLANES, QB), F32),) * 2)
    first = n_groups % 2
    if first:
        st = step(*st, slot=0)
    st = lax.fori_loop(0, n_groups // 2, lambda _, s: step(*step(*s, slot=first), slot=1 - first), st)
    for h2 in heads:
        values(h2, st[2], st[3], (n_groups - 1) % 2)


def _sb_prompt(qt, kp, vtp, bias):
    bz, _, nq_t, _, _ = qt.shape
    nk = kp.shape[1]
    assert (nq_t * TQ) % QB == 0
    nq = nq_t * TQ // QB
    b2 = bias.astype(F32) * 0.5
    hi = b2.astype(BF16)
    mid = (b2 - hi.astype(F32)).astype(BF16)
    lo = (b2 - hi.astype(F32) - mid.astype(F32)).astype(BF16)
    rows = jnp.stack([hi, mid, lo], axis=1)
    bias_rows = jnp.zeros((SB_HEADS, LANES, QB), BF16).at[:, :3, :].set(
        jnp.broadcast_to(rows[:, :, None], (SB_HEADS, 3, QB)))
    bias_rows = bias_rows.reshape(SB_HEADS // 2, 2, LANES, QB)
    ones_cols = jnp.zeros((GK, LANES), BF16).at[:, :3].set(1.0)
    pairs = SB_HEADS // 2
    return pl.pallas_call(
        functools.partial(_sb_prompt_kernel, nq=nq),
        grid=(bz * pairs,),
        in_specs=[
            pl.BlockSpec((None, 2, nq_t, SB_DH, TQ), lambda g: (g // pairs, g % pairs, 0, 0, 0)),
            pl.BlockSpec((None, nk, TK, LANES), lambda g: (g // pairs, 0, 0, g % pairs)),
            pl.BlockSpec((None, 2, nk, SB_DH, TK), lambda g: (g // pairs, g % pairs, 0, 0, 0)),
            pl.BlockSpec((None, 2, LANES, QB), lambda g: (g % pairs, 0, 0, 0)),
            _const_spec((GK, LANES)),
        ],
        out_specs=pl.BlockSpec((None, 2, nq_t, SB_DH, TQ), lambda g: (g // pairs, g % pairs, 0, 0, 0)),
        out_shape=jax.ShapeDtypeStruct((bz, SB_HEADS, nq_t, SB_DH, TQ), F32),
        scratch_shapes=[pltpu.VMEM((2, 2, QB // LANES, GK, LANES), F32),
                        pltpu.VMEM((2, 2, QB // LANES, GK, LANES), BF16)],
        compiler_params=pltpu.CompilerParams(dimension_semantics=("arbitrary",),
                                             vmem_limit_bytes=VMEM_LIMIT),
        name="sb_prompt",
    )(qt, kp, vtp, bias_rows, ones_cols)


def _sb_out_t_kernel(ot_ref, gt_ref, x_ref, w_ref, y_ref):
    o_t = ot_ref[...].reshape(SB_W, TQ)
    gated = (o_t * _silu(gt_ref[...].astype(F32))).T.astype(BF16)
    y_ref[0] = x_ref[0] + _dot(gated, w_ref[...])


def _sb_out_t(ot, gt, x, w_out):
    bz, _, nq, _, _ = ot.shape
    x_spec = pl.BlockSpec((1, TQ, D_MODEL), lambda b, l: (b, l, 0))
    return pl.pallas_call(
        _sb_out_t_kernel,
        grid=(bz, nq),
        in_specs=[
            pl.BlockSpec((None, SB_HEADS, None, SB_DH, TQ), lambda b, l: (b, 0, l, 0, 0)),
            pl.BlockSpec((None, None, SB_W, TQ), lambda b, l: (b, l, 0, 0)),
            x_spec,
            _const_spec((SB_W, D_MODEL)),
        ],
        out_specs=x_spec,
        out_shape=jax.ShapeDtypeStruct(x.shape, F32),
        compiler_params=pltpu.CompilerParams(dimension_semantics=("arbitrary", "arbitrary"),
                                             vmem_limit_bytes=VMEM_LIMIT),
        name="sb_out_prompt",
    )(ot, gt, x, w_out.astype(BF16))


PAGES_PER_STEP = 16


def _rev_cumprod_rows(m, carry):
    rows, n = m.shape
    rid = lax.broadcasted_iota(jnp.int32, (rows, n), 0) % SUBLANES
    p = m
    for s in (1, 2, 4):
        p = p * jnp.where(rid < SUBLANES - s, pltpu.roll(p, rows - s, axis=0), 1.0)
    nv = rows // SUBLANES
    p3 = p.reshape(nv, SUBLANES, n)
    out = [None] * nv
    for i in range(nv - 1, -1, -1):
        out[i] = p3[i] * carry
        carry = jnp.broadcast_to(out[i][0:1], (SUBLANES, n))
    return jnp.concatenate(out, axis=0), carry


def _sb_sample_kernel(pt_ref, q_ref, kn_ref, vn_ref, bias_ref, *refs, seq, n_steps):
    kpages = refs[:PAGES_PER_STEP]
    vpages = refs[PAGES_PER_STEP:2 * PAGES_PER_STEP]
    o_ref = refs[2 * PAGES_PER_STEP]
    qbd_scr, carry_scr, acc_scr, kt16_scr = refs[2 * PAGES_PER_STEP + 1:]
    step = pl.program_id(1)
    hq = SB_HEADS * seq

    def attend(u, carry, mask=None):
        w = jnp.exp2(jnp.minimum(u + bias_ref[...], Z2_MAX))
        if mask is not None:
            w = jnp.where(mask, w, 0.0)
        prod, carry = _rev_cumprod_rows(1.0 + w, carry)
        return (w / prod).T.astype(BF16), carry

    @pl.when(step == 0)
    def _():
        q = q_ref[...]
        qrow = jnp.concatenate([q] * SB_HEADS, axis=0)
        rh = lax.broadcasted_iota(jnp.int32, (hq, SB_W), 0) // seq
        lh = lax.broadcasted_iota(jnp.int32, (hq, SB_W), 1) // SB_DH
        qbd = jnp.where(rh == lh, qrow, 0.0).astype(BF16)
        qbd_scr[...] = qbd
        pad = jnp.zeros((PAGE - seq, SB_W), F32)
        k_new = jnp.concatenate([kn_ref[...], pad], axis=0).astype(BF16)
        v_new = jnp.concatenate([vn_ref[...], pad], axis=0).astype(BF16)
        lane = lax.broadcasted_iota(jnp.int32, (PAGE, hq), 1)
        krow = lax.broadcasted_iota(jnp.int32, (PAGE, hq), 0)
        a, carry = attend(_dot_nt(k_new, qbd), jnp.ones((SUBLANES, hq), F32), krow < (lane % seq))
        acc_scr[...] = _dot(a, v_new)
        carry_scr[...] = carry

    for i in range(PAGES_PER_STEP):
        kt16_scr[:, i * PAGE:(i + 1) * PAGE] = kpages[i][...].astype(BF16)
    z_all = _dot(qbd_scr[...], kt16_scr[...])
    carry = carry_scr[...]
    acc = acc_scr[...]
    for i in range(PAGES_PER_STEP):
        a, carry = attend(z_all[:, i * PAGE:(i + 1) * PAGE].T, carry)
        acc = acc + _dot_nt(a, vpages[i][...].astype(BF16))
    acc_scr[...] = acc
    carry_scr[...] = carry

    @pl.when(step == n_steps - 1)
    def _():
        o3 = acc.reshape(SB_HEADS, seq, SB_W)
        hsel = lax.broadcasted_iota(jnp.int32, (SB_HEADS, seq, SB_W), 0)
        lsel = lax.broadcasted_iota(jnp.int32, (SB_HEADS, seq, SB_W), 2) // SB_DH
        o_ref[...] = jnp.sum(jnp.where(hsel == lsel, o3, 0.0), axis=0)


def _sb_sample(q, k_new, v_new, cache_kt, cache_vt, page_table, bias):
    bz, seq, _ = q.shape
    n_pages = page_table.shape[1]
    assert n_pages % PAGES_PER_STEP == 0 and seq == SUBLANES
    n_steps = n_pages // PAGES_PER_STEP
    bias_lane = jnp.repeat(bias.astype(F32) * LOG2E, seq).reshape(1, SB_HEADS * seq)
    tok_spec = pl.BlockSpec((None, seq, SB_W), lambda b, s, pt: (b, 0, 0))

    def page_spec(i):
        def index(b, s, pt):
            return (pt[b * n_pages + n_pages - 1 - (s * PAGES_PER_STEP + i)], 0, 0)
        return pl.BlockSpec((None, SB_W, PAGE), index)

    pages = [page_spec(i) for i in range(PAGES_PER_STEP)]
    hq = SB_HEADS * seq
    return pl.pallas_call(
        functools.partial(_sb_sample_kernel, seq=seq, n_steps=n_steps),
        grid_spec=pltpu.PrefetchScalarGridSpec(
            num_scalar_prefetch=1,
            grid=(bz, n_steps),
            in_specs=[tok_spec, tok_spec, tok_spec, pl.BlockSpec((1, hq), lambda b, s, pt: (0, 0))] + pages + pages,
            out_specs=tok_spec,
            scratch_shapes=[pltpu.VMEM((hq, SB_W), BF16), pltpu.VMEM((SUBLANES, hq), F32),
                            pltpu.VMEM((hq, SB_W), F32), pltpu.VMEM((SB_W, PAGES_PER_STEP * PAGE), BF16)],
        ),
        out_shape=jax.ShapeDtypeStruct((bz, seq, SB_W), F32),
        compiler_params=pltpu.CompilerParams(dimension_semantics=("arbitrary", "arbitrary"),
                                             vmem_limit_bytes=VMEM_LIMIT),
        name="sb_sample",
    )(page_table.reshape(-1), q, k_new, v_new, bias_lane,
      *([cache_kt] * PAGES_PER_STEP), *([cache_vt] * PAGES_PER_STEP))


def _sb_out_kernel(o_ref, g_ref, x_ref, w_ref, y_ref):
    gated = (o_ref[...] * _silu(g_ref[...])).astype(BF16)
    y_ref[...] = x_ref[...] + _dot(gated, w_ref[...])


def _sb_out(o, g, x, w_out, *, tm):
    n = o.shape[0]
    spec = pl.BlockSpec((tm, D_MODEL), lambda i: (i, 0))
    return pl.pallas_call(
        _sb_out_kernel,
        grid=(n // tm,),
        in_specs=[spec, spec, spec, _const_spec((SB_W, D_MODEL))],
        out_specs=spec,
        out_shape=jax.ShapeDtypeStruct((n, D_MODEL), F32),
        compiler_params=pltpu.CompilerParams(dimension_semantics=("arbitrary",), vmem_limit_bytes=VMEM_LIMIT),
        name="sb_out_sample",
    )(o, g, x, w_out.astype(BF16))


def kernel(x_prompt, x_sample, state_gla, cache_k, cache_v, page_table, norm_gain, w_in_a, w_alpha_up, b_alpha, onorm_a, w_out_a, w_in_b, qnorm_b, knorm_b, sb_bias, w_out_b):
    bz, seq_p, _ = x_prompt.shape
    dec_b, dec_s, _ = x_sample.shape

    gla = functools.partial(_gla_layer, gain=norm_gain[0], w_in=w_in_a[0], w_up=w_alpha_up[0], b_a=b_alpha[0],
                            onorm=onorm_a[0], w_out=w_out_a[0])
    xp, st_p = gla(x_prompt, None, nb=1, tm=4 * GLA_CHUNK, chunk=GLA_CHUNK)
    xs, st_s = gla(x_sample, state_gla[0], nb=GLA_GROUP // dec_s, tm=dec_s, chunk=dec_s)

    k_p, v_p, qt, kp, vtp, gt = _sb_in(xp, norm_gain[1], w_in_b[0], qnorm_b[0], knorm_b[0], tm=TQ, prompt=True)
    ot = _sb_prompt(qt, kp, vtp, sb_bias[0])
    yp = _sb_out_t(ot, gt, xp, w_out_b[0])

    xs_flat = xs.reshape(1, dec_b * dec_s, D_MODEL)
    k_s, v_s, q_s, g_s = _sb_in(xs_flat, norm_gain[1], w_in_b[0], qnorm_b[0], knorm_b[0], tm=TQ, prompt=False)
    per_seq = (dec_b, dec_s, SB_W)
    n_phys = cache_k.shape[1]
    cache_kt = jnp.transpose(cache_k[0], (0, 2, 3, 1)).reshape(n_phys, SB_W, PAGE)
    cache_vt = jnp.transpose(cache_v[0], (0, 2, 3, 1)).reshape(n_phys, SB_W, PAGE)
    o_s = _sb_sample(q_s.reshape(per_seq), k_s.reshape(per_seq), v_s.reshape(per_seq),
                     cache_kt, cache_vt, page_table, sb_bias[0])
    ys = _sb_out(o_s.reshape(dec_b * dec_s, SB_W), g_s[0], xs_flat[0], w_out_b[0], tm=TQ)

    heads = (SB_HEADS, SB_DH)

    def from_feature_major(t):
        return jnp.transpose(t.reshape(bz, *heads, seq_p), (0, 3, 1, 2))[None]

    return (yp, ys.reshape(dec_b, dec_s, D_MODEL), st_p[None], st_s[None],
            from_feature_major(k_p), from_feature_major(v_p),
            k_s.reshape(1, dec_b, dec_s, *heads), v_s.reshape(1, dec_b, dec_s, *heads))
```

```python
import functools
import math

import jax
import jax.numpy as jnp
import numpy as np
from jax import lax
from jax.experimental import pallas as pl
from jax.experimental.pallas import tpu as pltpu

EPS = 1e-6
LOG2E = math.log2(math.e)

D_MODEL = 1024
GLA_HEADS = 4
GLA_DK = 128
GLA_DV = 256
GLA_KEY_W = GLA_HEADS * GLA_DK
GLA_VAL_W = GLA_HEADS * GLA_DV
GLA_RANK = 16
GLA_TAU = 16.0
GLA_CHUNK = 64
GLA_GROUP = 64
SB_HEADS = 16
SB_DH = 64
SB_W = SB_HEADS * SB_DH
PAGE = 128
LANES = 128
SUBLANES = 8
VMEM_LIMIT = 56 * 1024 * 1024

TQ = 256
TK = 128
QB = 512
GK = 512
Z2_MAX = 80.0

BF16 = jnp.bfloat16
F32 = jnp.float32


def _dot(a, b):
    return jnp.dot(a, b, preferred_element_type=F32)


def _dot_nt(a, b):
    return lax.dot_general(a, b, (((1,), (1,)), ((), ())), preferred_element_type=F32)


def _split2(x):
    hi = x.astype(BF16)
    lo = (x - hi.astype(F32)).astype(BF16)
    return hi, lo


def _log_sigmoid(x):
    return jnp.minimum(x, 0.0) - jnp.log(1.0 + jnp.exp(-jnp.abs(x)))


def _silu(x):
    return x / (1.0 + jnp.exp(-x))


def _rms_rows(x, gain):
    ms = jnp.mean(x * x, axis=-1, keepdims=True)
    return x * lax.rsqrt(ms + EPS) * gain


def _const_spec(shape):
    zeros = (0,) * len(shape)
    return pl.BlockSpec(shape, lambda *_: zeros, pipeline_mode=pl.Buffered(1))


def _gla_kernel(*refs, nb, tm, chunk, has_s0):
    if has_s0:
        (x_ref, s0_ref, gain_ref, wm_ref, wkt_ref, wac_ref, wact_ref, wup_ref, wupt_ref, ba_ref, bat_ref,
         on_ref, wout_ref, lcum_ref, mlater_ref, y_ref, sout_ref, s_scr, o_scr) = refs
    else:
        (x_ref, gain_ref, wm_ref, wkt_ref, wac_ref, wact_ref, wup_ref, wupt_ref, ba_ref, bat_ref,
         on_ref, wout_ref, lcum_ref, mlater_ref, y_ref, sout_ref, s_scr, o_scr) = refs
        s0_ref = None
    m = nb * tm
    step = pl.program_id(1)

    @pl.when(step == 0)
    def _():
        if has_s0:
            s_scr[...] = s0_ref[...]
        else:
            s_scr[...] = jnp.zeros_like(s_scr)

    x = x_ref[...].reshape(m, D_MODEL)
    h = _rms_rows(x, gain_ref[...]).astype(BF16)
    u = _dot(h, wm_ref[...])
    q = u[:, :GLA_KEY_W] * (GLA_DK ** -0.5)
    k = u[:, GLA_KEY_W:2 * GLA_KEY_W]
    v16 = u[:, 2 * GLA_KEY_W:2 * GLA_KEY_W + GLA_VAL_W].astype(BF16)
    gate = u[:, 2 * GLA_KEY_W + GLA_VAL_W:]

    code = _dot(h, wac_ref[...]).astype(BF16)
    la = _log_sigmoid(_dot(code, wup_ref[...]) + ba_ref[...]) * (1.0 / GLA_TAU)
    code_t = _dot_nt(wact_ref[...], h).astype(BF16)
    la_t = _log_sigmoid(_dot(wupt_ref[...], code_t) + bat_ref[...]) * (1.0 / GLA_TAU)
    k_t = k.T if m % LANES == 0 else _dot_nt(wkt_ref[...], h)

    la_hi, la_lo = _split2(la)
    b = _dot(lcum_ref[...], la_hi) + _dot(lcum_ref[...], la_lo)
    qd = (q * jnp.exp(b)).astype(BF16)
    kd = (k * jnp.exp(-b)).astype(BF16)
    lat_hi, lat_lo = _split2(la_t)
    rest_t = _dot(lat_hi, mlater_ref[...]) + _dot(lat_lo, mlater_ref[...])
    kl_t = k_t * jnp.exp(rest_t)

    lb = min(LANES, m)
    n_groups = m // GLA_GROUP
    n_sub = GLA_GROUP // chunk
    ri = lax.broadcasted_iota(jnp.int32, (GLA_GROUP, GLA_GROUP), 0)
    ci = lax.broadcasted_iota(jnp.int32, (GLA_GROUP, GLA_GROUP), 1)
    att_mask = ((ri // chunk) == (ci // chunk)) & (ci <= ri)

    for g in range(n_groups):
        r0 = g * GLA_GROUP
        rows = slice(r0, r0 + GLA_GROUP)
        blk = (r0 // lb) * lb
        cols = slice(blk, blk + lb)
        for hd in range(GLA_HEADS):
            fk = slice(hd * GLA_DK, (hd + 1) * GLA_DK)
            fv = slice(hd * GLA_DV, (hd + 1) * GLA_DV)
            qd_g = qd[rows, fk]
            att = jnp.where(att_mask, _dot_nt(qd_g, kd[rows, fk]), 0.0).astype(BF16)
            o = _dot(att, v16[rows, fv])
            for j in range(n_sub):
                sub = (r0 + j * chunk) // chunk
                si = (r0 + j * chunk) // tm
                s_old = s_scr[si, hd]
                o_st = _dot(qd_g, s_old.astype(BF16))
                if n_sub > 1:
                    rsel = lax.broadcasted_iota(jnp.int32, (GLA_GROUP, GLA_DV), 0)
                    o_st = jnp.where((rsel + r0) // chunk == sub, o_st, 0.0)
                o = o + o_st
                csel = (lax.broadcasted_iota(jnp.int32, (GLA_DK, lb), 1) + blk) // chunk == sub
                la_sel = jnp.where(csel, la_t[fk, cols], 0.0)
                kl_sel = jnp.where(csel, kl_t[fk, cols], 0.0).astype(BF16)
                decay = jnp.exp(jnp.sum(la_sel, axis=1, keepdims=True))
                s_scr[si, hd] = s_old * decay + _dot(kl_sel, v16[cols, fv])
            o_scr[rows, fv] = o

    y = x
    for hd in range(GLA_HEADS):
        fv = slice(hd * GLA_DV, (hd + 1) * GLA_DV)
        on = _rms_rows(o_scr[:, fv], on_ref[...])
        gated = (on * _silu(gate[:, fv])).astype(BF16)
        y = y + _dot(gated, wout_ref[fv, :])
    y_ref[...] = y.reshape(nb, tm, D_MODEL)

    @pl.when(step == pl.num_programs(1) - 1)
    def _():
        sout_ref[...] = s_scr[...]


def _block_structure(m, chunk):
    idx = np.arange(m)
    same = (idx[:, None] // chunk) == (idx[None, :] // chunk)
    lcum = same & (idx[None, :] <= idx[:, None])
    later = same & (idx[:, None] > idx[None, :])
    return jnp.asarray(lcum, BF16), jnp.asarray(later, BF16)


def _gla_layer(x, s0, gain, w_in, w_up, b_a, onorm, w_out, *, nb, tm, chunk):
    bz, length, _ = x.shape
    has_s0 = s0 is not None
    m = nb * tm
    assert bz % nb == 0 and length % tm == 0 and m % GLA_GROUP == 0
    assert chunk == GLA_GROUP or (chunk == tm and m == GLA_GROUP)
    wm = w_in[:, :2 * GLA_KEY_W + 2 * GLA_VAL_W].astype(BF16)
    wkt = w_in[:, GLA_KEY_W:2 * GLA_KEY_W].T.astype(BF16)
    wac = jnp.pad(w_in[:, 2 * GLA_KEY_W + 2 * GLA_VAL_W:], ((0, 0), (0, LANES - GLA_RANK))).astype(BF16)
    wup = jnp.pad(w_up, ((0, LANES - GLA_RANK), (0, 0))).astype(BF16)
    lcum, later = _block_structure(m, chunk)
    consts = [gain.reshape(1, D_MODEL), wm, wkt, wac, wac.T, wup, wup.T, b_a.reshape(1, GLA_KEY_W),
              b_a.reshape(GLA_KEY_W, 1), onorm.reshape(1, GLA_DV), w_out.astype(BF16), lcum, later]
    state_shape = (nb, GLA_HEADS, GLA_DK, GLA_DV)
    state_spec = pl.BlockSpec(state_shape, lambda b, l: (b, 0, 0, 0))
    x_spec = pl.BlockSpec((nb, tm, D_MODEL), lambda b, l: (b, l, 0))
    in_specs = [x_spec] + ([state_spec] if has_s0 else []) + [_const_spec(c.shape) for c in consts]
    args = [x] + ([s0] if has_s0 else []) + consts
    return pl.pallas_call(
        functools.partial(_gla_kernel, nb=nb, tm=tm, chunk=chunk, has_s0=has_s0),
        grid=(bz // nb, length // tm),
        in_specs=in_specs,
        out_specs=[x_spec, state_spec],
        out_shape=[jax.ShapeDtypeStruct(x.shape, F32),
                   jax.ShapeDtypeStruct((bz, GLA_HEADS, GLA_DK, GLA_DV), F32)],
        scratch_shapes=[pltpu.VMEM(state_shape, F32), pltpu.VMEM((m, GLA_VAL_W), F32)],
        compiler_params=pltpu.CompilerParams(dimension_semantics=("arbitrary", "arbitrary"),
                                             vmem_limit_bytes=VMEM_LIMIT),
        name="gla_layer_s0" if has_s0 else "gla_layer",
    )(*args)


def _key_permutation():
    p = np.arange(TK)
    return (TK // SUBLANES) * (p % SUBLANES) + p // SUBLANES


def _head_norm_t(t, gain_col, tokens):
    t3 = t.reshape(SB_HEADS, SB_DH, tokens)
    ms = jnp.mean(t3 * t3, axis=1, keepdims=True)
    return (t3 * lax.rsqrt(ms + EPS)).reshape(SB_W, tokens) * gain_col


def _sb_in_kernel(*refs, tm, prompt):
    if prompt:
        (x_ref, gain_ref, wt_ref, qg_ref, kg_ref, p_ref, pt_ref,
         k_ref, v_ref, qt_ref, kp_ref, vtp_ref, gt_ref) = refs
    else:
        (x_ref, gain_ref, wt_ref, qg_ref, kg_ref, k_ref, v_ref, q_ref, g_ref) = refs
    x = x_ref[0]
    h = _rms_rows(x, gain_ref[...]).astype(BF16)
    u_t = _dot_nt(wt_ref[...], h)
    q_t = _head_norm_t(u_t[:SB_W], qg_ref[...], tm) * (SB_DH ** -0.5 * (0.5 if prompt else LOG2E))
    k_t = _head_norm_t(u_t[SB_W:2 * SB_W], kg_ref[...], tm)
    v_t = u_t[2 * SB_W:3 * SB_W]
    g_t = u_t[3 * SB_W:]
    if prompt:
        k_ref[0] = k_t
        v_ref[0] = v_t
        qt_ref[...] = q_t.astype(BF16).reshape(SB_HEADS, SB_DH, tm)
        gt_ref[...] = g_t.astype(BF16)
        kp_ref[...] = _dot_nt(p_ref[...], k_t.astype(BF16)).astype(BF16).reshape(tm // TK, TK, SB_W)
        vp_t = _dot(v_t.astype(BF16), pt_ref[...])
        for j in range(tm // TK):
            vtp_ref[:, j] = vp_t[:, j * TK:(j + 1) * TK].astype(BF16).reshape(SB_HEADS, SB_DH, TK)
    else:
        k_ref[0] = k_t.T
        v_ref[0] = v_t.T
        q_ref[0] = q_t.T
        g_ref[0] = g_t.T


def _sb_in(x, gain, w_in, qn, kn, *, tm, prompt):
    bz, length, _ = x.shape
    assert length % tm == 0 and tm % TK == 0
    nt = length // tm
    wt = w_in.T.astype(BF16)
    qg = jnp.tile(qn, SB_HEADS).reshape(SB_W, 1)
    kg = jnp.tile(kn, SB_HEADS).reshape(SB_W, 1)
    consts = [gain.reshape(1, D_MODEL), wt, qg, kg]
    nat_spec = pl.BlockSpec((1, tm, SB_W), lambda b, l: (b, l, 0))
    nat_shape = jax.ShapeDtypeStruct((bz, length, SB_W), F32)
    if prompt:
        assert tm == TQ
        perm = np.concatenate([_key_permutation() + j * TK for j in range(tm // TK)])
        pt = np.zeros((tm, tm), np.float32)
        pt[perm, np.arange(tm)] = 1.0
        consts += [jnp.asarray(pt.T, BF16), jnp.asarray(pt, BF16)]
        t_spec = pl.BlockSpec((1, SB_W, tm), lambda b, l: (b, 0, l))
        t_shape = jax.ShapeDtypeStruct((bz, SB_W, length), F32)
        out_specs = [
            t_spec, t_spec,
            pl.BlockSpec((None, SB_HEADS, None, SB_DH, tm), lambda b, l: (b, 0, l, 0, 0)),
            pl.BlockSpec((None, tm // TK, TK, SB_W), lambda b, l: (b, l, 0, 0)),
            pl.BlockSpec((None, SB_HEADS, tm // TK, SB_DH, TK), lambda b, l: (b, 0, l, 0, 0)),
            pl.BlockSpec((None, None, SB_W, tm), lambda b, l: (b, l, 0, 0)),
        ]
        out_shape = [
            t_shape, t_shape,
            jax.ShapeDtypeStruct((bz, SB_HEADS, nt, SB_DH, tm), BF16),
            jax.ShapeDtypeStruct((bz, length // TK, TK, SB_W), BF16),
            jax.ShapeDtypeStruct((bz, SB_HEADS, length // TK, SB_DH, TK), BF16),
            jax.ShapeDtypeStruct((bz, nt, SB_W, tm), BF16),
        ]
    else:
        out_specs = [nat_spec] * 4
        out_shape = [nat_shape] * 4
    return pl.pallas_call(
        functools.partial(_sb_in_kernel, tm=tm, prompt=prompt),
        grid=(bz, nt),
        in_specs=[nat_spec] + [_const_spec(c.shape) for c in consts],
        out_specs=out_specs,
        out_shape=out_shape,
        compiler_params=pltpu.CompilerParams(dimension_semantics=("arbitrary", "arbitrary"),
                                             vmem_limit_bytes=VMEM_LIMIT),
        name="sb_in_prompt" if prompt else "sb_in_sample",
    )(x, *consts)


def _shift_rows_up(x, s, rid):
    return jnp.where(rid < SUBLANES - s, pltpu.roll(x, SUBLANES - s, axis=0), 1.0)


def _sb_tile(u, carry, rid, mask=None):
    n = u.shape[-1]
    rho = 0.5 - 0.5 * jnp.tanh(u)
    if mask is not None:
        rho = jnp.where(mask, rho, 1.0)
    nv = TK // SUBLANES
    r3 = rho.reshape(nv, SUBLANES, n)
    run = [None] * nv
    run[nv - 1] = r3[nv - 1]
    for i in range(nv - 2, -1, -1):
        run[i] = r3[i] * run[i + 1]
    tot = run[0]
    later = _shift_rows_up(tot, 1, rid)
    later = later * _shift_rows_up(later, 1, rid)
    later = later * _shift_rows_up(later, 2, rid)
    later = later * _shift_rows_up(later, 4, rid)
    grp = later * carry
    run.append(jnp.ones_like(tot))
    diff = jnp.concatenate([run[i + 1] - run[i] for i in range(nv)], axis=0).astype(BF16)
    grp2 = jnp.concatenate([grp, grp], axis=0).astype(BF16)
    a = (diff.reshape(nv // 2, 2 * SUBLANES, n) * grp2[None]).reshape(TK, n)
    new_carry = jnp.broadcast_to((grp * tot)[0:1], (SUBLANES, n))
    return a, new_carry


def _sb_prompt_kernel(qt_ref, kp_ref, vtp_ref, bias_ref, ones_ref, ot_ref, u_scr, a_scr, acc_scr, *, nq):
    rid = lax.broadcasted_iota(jnp.int32, (SUBLANES, LANES), 0)
    prow = lax.broadcasted_iota(jnp.int32, (TK, LANES), 0)
    key_in_tile = (TK // SUBLANES) * (prow % SUBLANES) + prow // SUBLANES
    qlane = lax.broadcasted_iota(jnp.int32, (TK, LANES), 1)
    zeros_q = jnp.zeros((SB_DH, QB), BF16)
    tiles = GK // TK
    halves = QB // TQ
    strips = QB // LANES
    n_groups = nq * (nq + 1) // 2

    heads = range(2)

    def scores(h2, qb, g, slot):
        q = jnp.concatenate([qt_ref[h2, halves * qb + i] for i in range(halves)], axis=1)
        parts = [q, zeros_q] if h2 == 0 else [zeros_q, q]
        qaug = jnp.concatenate(parts + [bias_ref[h2]], axis=0)
        keys = kp_ref[pl.ds(g * tiles, tiles)].reshape(GK, LANES)
        u = _dot(jnp.concatenate([keys, ones_ref[...]], axis=1), qaug)
        for c in range(strips):
            u_scr[slot, h2, c] = u[:, c * LANES:(c + 1) * LANES]

    def weights(h2, slot, carry, diag):
        out = []
        for c in range(strips):
            cs = slice(c * LANES, (c + 1) * LANES)
            car = carry[:, cs]
            for t in range(tiles - 1, -1, -1):
                rows = slice(t * TK, (t + 1) * TK)
                lead = t * TK - c * LANES
                if diag and lead >= LANES:
                    a_scr[slot, h2, c, rows] = jnp.zeros((TK, LANES), BF16)
                    continue
                mask = (key_in_tile + lead) < qlane if diag and lead > -TK else None
                a, car = _sb_tile(u_scr[slot, h2, c, rows], car, rid, mask)
                a_scr[slot, h2, c, rows] = a
            out.append(car)
        return jnp.concatenate(out, axis=1)

    def values(h2, qb, g, slot):
        vt = jnp.concatenate([vtp_ref[h2, g * tiles + t] for t in range(tiles)], axis=1)
        part = _dot(vt, jnp.concatenate([a_scr[slot, h2, c] for c in range(strips)], axis=1))
        for i in range(halves):
            total = jnp.where(g == qb, 0.0, acc_scr[h2, halves * qb + i]) + part[:, i * TQ:(i + 1) * TQ]
            acc_scr[h2, halves * qb + i] = total
            ot_ref[h2, halves * qb + i] = total.astype(BF16)

    def step(qb, g, pqb, pg, carries, *, slot):
        last = g == 0
        nqb = jnp.minimum(jnp.where(last, qb + 1, qb), nq - 1)
        ng = jnp.where(last, nqb, g - 1)

        def body(diag):
            def run(carries):
                ones = jnp.ones((SUBLANES, QB), F32)
                carries = tuple(weights(h2, slot, ones if diag else carries[h2], diag) for h2 in heads)
                for h2 in heads:
                    values(h2, pqb, pg, 1 - slot)
                    scores(h2, nqb, ng, 1 - slot)
                return carries
            return run

        return nqb, ng, qb, g, lax.cond(g == qb, body(True), body(False), carries)

    zero = jnp.zeros((), jnp.int32)
    for h2 in heads:
        scores(h2, zero, zero, 0)
    a_scr[1] = jnp.zeros((2, strips, GK, LANES), BF16)
    acc_scr[...] = jnp.zeros_like(acc_scr)
    st = (zero, zero, zero, zero, (jnp.ones((SUBLANES, QB), F32),) * 2)
    first = n_groups % 2
    if first:
        st = step(*st, slot=0)
    st = lax.fori_loop(0, n_groups // 2, lambda _, s: step(*step(*s, slot=first), slot=1 - first), st)
    for h2 in heads:
        values(h2, st[2], st[3], (n_groups - 1) % 2)


def _sb_prompt(qt, kp, vtp, bias):
    bz, _, nq_t, _, _ = qt.shape
    nk = kp.shape[1]
    assert (nq_t * TQ) % QB == 0
    nq = nq_t * TQ // QB
    b2 = bias.astype(F32) * 0.5
    hi = b2.astype(BF16)
    mid = (b2 - hi.astype(F32)).astype(BF16)
    lo = (b2 - hi.astype(F32) - mid.astype(F32)).astype(BF16)
    rows = jnp.stack([hi, mid, lo], axis=1)
    bias_rows = jnp.zeros((SB_HEADS, LANES, QB), BF16).at[:, :3, :].set(
        jnp.broadcast_to(rows[:, :, None], (SB_HEADS, 3, QB)))
    bias_rows = bias_rows.reshape(SB_HEADS // 2, 2, LANES, QB)
    ones_cols = jnp.zeros((GK, LANES), BF16).at[:, :3].set(1.0)
    pairs = SB_HEADS // 2
    return pl.pallas_call(
        functools.partial(_sb_prompt_kernel, nq=nq),
        grid=(bz * pairs,),
        in_specs=[
            pl.BlockSpec((None, 2, nq_t, SB_DH, TQ), lambda g: (g // pairs, g % pairs, 0, 0, 0)),
            pl.BlockSpec((None, nk, TK, LANES), lambda g: (g // pairs, 0, 0, g % pairs)),
            pl.BlockSpec((None, 2, nk, SB_DH, TK), lambda g: (g // pairs, g % pairs, 0, 0, 0)),
            pl.BlockSpec((None, 2, LANES, QB), lambda g: (g % pairs, 0, 0, 0)),
            _const_spec((GK, LANES)),
        ],
        out_specs=pl.BlockSpec((None, 2, nq_t, SB_DH, TQ), lambda g: (g // pairs, g % pairs, 0, 0, 0)),
        out_shape=jax.ShapeDtypeStruct((bz, SB_HEADS, nq_t, SB_DH, TQ), BF16),
        scratch_shapes=[pltpu.VMEM((2, 2, QB // LANES, GK, LANES), F32),
                        pltpu.VMEM((2, 2, QB // LANES, GK, LANES), BF16),
                        pltpu.VMEM((2, nq_t, SB_DH, TQ), F32)],
        compiler_params=pltpu.CompilerParams(dimension_semantics=("arbitrary",),
                                             vmem_limit_bytes=VMEM_LIMIT),
        name="sb_prompt",
    )(qt, kp, vtp, bias_rows, ones_cols)


def _sb_out_t_kernel(ot_ref, gt_ref, x_ref, w_ref, y_ref):
    o_t = ot_ref[...].reshape(SB_W, TQ).astype(F32)
    gated = (o_t * _silu(gt_ref[...].astype(F32))).T.astype(BF16)
    y_ref[0] = x_ref[0] + _dot(gated, w_ref[...])


def _sb_out_t(ot, gt, x, w_out):
    bz, _, nq, _, _ = ot.shape
    x_spec = pl.BlockSpec((1, TQ, D_MODEL), lambda b, l: (b, l, 0))
    return pl.pallas_call(
        _sb_out_t_kernel,
        grid=(bz, nq),
        in_specs=[
            pl.BlockSpec((None, SB_HEADS, None, SB_DH, TQ), lambda b, l: (b, 0, l, 0, 0)),
            pl.BlockSpec((None, None, SB_W, TQ), lambda b, l: (b, l, 0, 0)),
            x_spec,
            _const_spec((SB_W, D_MODEL)),
        ],
        out_specs=x_spec,
        out_shape=jax.ShapeDtypeStruct(x.shape, F32),
        compiler_params=pltpu.CompilerParams(dimension_semantics=("arbitrary", "arbitrary"),
                                             vmem_limit_bytes=VMEM_LIMIT),
        name="sb_out_prompt",
    )(ot, gt, x, w_out.astype(BF16))


PAGES_PER_STEP = 16


def _rev_cumprod_rows(m, carry):
    rows, n = m.shape
    rid = lax.broadcasted_iota(jnp.int32, (rows, n), 0) % SUBLANES
    p = m
    for s in (1, 2, 4):
        p = p * jnp.where(rid < SUBLANES - s, pltpu.roll(p, rows - s, axis=0), 1.0)
    nv = rows // SUBLANES
    p3 = p.reshape(nv, SUBLANES, n)
    out = [None] * nv
    for i in range(nv - 1, -1, -1):
        out[i] = p3[i] * carry
        carry = jnp.broadcast_to(out[i][0:1], (SUBLANES, n))
    return jnp.concatenate(out, axis=0), carry


def _sb_sample_kernel(pt_ref, q_ref, kn_ref, vn_ref, bias_ref, *refs, seq, n_steps):
    kpages = refs[:PAGES_PER_STEP]
    vpages = refs[PAGES_PER_STEP:2 * PAGES_PER_STEP]
    o_ref = refs[2 * PAGES_PER_STEP]
    qbd_scr, carry_scr, acc_scr, kt16_scr = refs[2 * PAGES_PER_STEP + 1:]
    step = pl.program_id(1)
    hq = SB_HEADS * seq

    def attend(u, carry, mask=None):
        w = jnp.exp2(jnp.minimum(u + bias_ref[...], Z2_MAX))
        if mask is not None:
            w = jnp.where(mask, w, 0.0)
        prod, carry = _rev_cumprod_rows(1.0 + w, carry)
        return (w / prod).T.astype(BF16), carry

    @pl.when(step == 0)
    def _():
        q = q_ref[...]
        qrow = jnp.concatenate([q] * SB_HEADS, axis=0)
        rh = lax.broadcasted_iota(jnp.int32, (hq, SB_W), 0) // seq
        lh = lax.broadcasted_iota(jnp.int32, (hq, SB_W), 1) // SB_DH
        qbd = jnp.where(rh == lh, qrow, 0.0).astype(BF16)
        qbd_scr[...] = qbd
        pad = jnp.zeros((PAGE - seq, SB_W), F32)
        k_new = jnp.concatenate([kn_ref[...], pad], axis=0).astype(BF16)
        v_new = jnp.concatenate([vn_ref[...], pad], axis=0).astype(BF16)
        lane = lax.broadcasted_iota(jnp.int32, (PAGE, hq), 1)
        krow = lax.broadcasted_iota(jnp.int32, (PAGE, hq), 0)
        a, carry = attend(_dot_nt(k_new, qbd), jnp.ones((SUBLANES, hq), F32), krow < (lane % seq))
        acc_scr[...] = _dot(a, v_new)
        carry_scr[...] = carry

    for i in range(PAGES_PER_STEP):
        kt16_scr[:, i * PAGE:(i + 1) * PAGE] = kpages[i][...].astype(BF16)
    z_all = _dot(qbd_scr[...], kt16_scr[...])
    carry = carry_scr[...]
    acc = acc_scr[...]
    for i in range(PAGES_PER_STEP):
        a, carry = attend(z_all[:, i * PAGE:(i + 1) * PAGE].T, carry)
        acc = acc + _dot_nt(a, vpages[i][...].astype(BF16))
    acc_scr[...] = acc
    carry_scr[...] = carry

    @pl.when(step == n_steps - 1)
    def _():
        o3 = acc.reshape(SB_HEADS, seq, SB_W)
        hsel = lax.broadcasted_iota(jnp.int32, (SB_HEADS, seq, SB_W), 0)
        lsel = lax.broadcasted_iota(jnp.int32, (SB_HEADS, seq, SB_W), 2) // SB_DH
        o_ref[...] = jnp.sum(jnp.where(hsel == lsel, o3, 0.0), axis=0)


def _sb_sample(q, k_new, v_new, cache_kt, cache_vt, page_table, bias):
    bz, seq, _ = q.shape
    n_pages = page_table.shape[1]
    assert n_pages % PAGES_PER_STEP == 0 and seq == SUBLANES
    n_steps = n_pages // PAGES_PER_STEP
    bias_lane = jnp.repeat(bias.astype(F32) * LOG2E, seq).reshape(1, SB_HEADS * seq)
    tok_spec = pl.BlockSpec((None, seq, SB_W), lambda b, s, pt: (b, 0, 0))

    def page_spec(i):
        def index(b, s, pt):
            return (pt[b * n_pages + n_pages - 1 - (s * PAGES_PER_STEP + i)], 0, 0)
        return pl.BlockSpec((None, SB_W, PAGE), index)

    pages = [page_spec(i) for i in range(PAGES_PER_STEP)]
    hq = SB_HEADS * seq
    return pl.pallas_call(
        functools.partial(_sb_sample_kernel, seq=seq, n_steps=n_steps),
        grid_spec=pltpu.PrefetchScalarGridSpec(
            num_scalar_prefetch=1,
            grid=(bz, n_steps),
            in_specs=[tok_spec, tok_spec, tok_spec, pl.BlockSpec((1, hq), lambda b, s, pt: (0, 0))] + pages + pages,
            out_specs=tok_spec,
            scratch_shapes=[pltpu.VMEM((hq, SB_W), BF16), pltpu.VMEM((SUBLANES, hq), F32),
                            pltpu.VMEM((hq, SB_W), F32), pltpu.VMEM((SB_W, PAGES_PER_STEP * PAGE), BF16)],
        ),
        out_shape=jax.ShapeDtypeStruct((bz, seq, SB_W), F32),
        compiler_params=pltpu.CompilerParams(dimension_semantics=("arbitrary", "arbitrary"),
                                             vmem_limit_bytes=VMEM_LIMIT),
        name="sb_sample",
    )(page_table.reshape(-1), q, k_new, v_new, bias_lane,
      *([cache_kt] * PAGES_PER_STEP), *([cache_vt] * PAGES_PER_STEP))


def _sb_out_kernel(o_ref, g_ref, x_ref, w_ref, y_ref):
    gated = (o_ref[...] * _silu(g_ref[...])).astype(BF16)
    y_ref[...] = x_ref[...] + _dot(gated, w_ref[...])


def _sb_out(o, g, x, w_out, *, tm):
    n = o.shape[0]
    spec = pl.BlockSpec((tm, D_MODEL), lambda i: (i, 0))
    return pl.pallas_call(
        _sb_out_kernel,
        grid=(n // tm,),
        in_specs=[spec, spec, spec, _const_spec((SB_W, D_MODEL))],
        out_specs=spec,
        out_shape=jax.ShapeDtypeStruct((n, D_MODEL), F32),
        compiler_params=pltpu.CompilerParams(dimension_semantics=("arbitrary",), vmem_limit_bytes=VMEM_LIMIT),
        name="sb_out_sample",
    )(o, g, x, w_out.astype(BF16))


def kernel(x_prompt, x_sample, state_gla, cache_k, cache_v, page_table, norm_gain, w_in_a, w_alpha_up, b_alpha, onorm_a, w_out_a, w_in_b, qnorm_b, knorm_b, sb_bias, w_out_b):
    bz, seq_p, _ = x_prompt.shape
    dec_b, dec_s, _ = x_sample.shape

    gla = functools.partial(_gla_layer, gain=norm_gain[0], w_in=w_in_a[0], w_up=w_alpha_up[0], b_a=b_alpha[0],
                            onorm=onorm_a[0], w_out=w_out_a[0])
    xp, st_p = gla(x_prompt, None, nb=1, tm=4 * GLA_CHUNK, chunk=GLA_CHUNK)
    xs, st_s = gla(x_sample, state_gla[0], nb=GLA_GROUP // dec_s, tm=dec_s, chunk=dec_s)

    k_p, v_p, qt, kp, vtp, gt = _sb_in(xp, norm_gain[1], w_in_b[0], qnorm_b[0], knorm_b[0], tm=TQ, prompt=True)
    ot = _sb_prompt(qt, kp, vtp, sb_bias[0])
    yp = _sb_out_t(ot, gt, xp, w_out_b[0])

    xs_flat = xs.reshape(1, dec_b * dec_s, D_MODEL)
    k_s, v_s, q_s, g_s = _sb_in(xs_flat, norm_gain[1], w_in_b[0], qnorm_b[0], knorm_b[0], tm=TQ, prompt=False)
    per_seq = (dec_b, dec_s, SB_W)
    n_phys = cache_k.shape[1]
    cache_kt = jnp.transpose(cache_k[0], (0, 2, 3, 1)).reshape(n_phys, SB_W, PAGE)
    cache_vt = jnp.transpose(cache_v[0], (0, 2, 3, 1)).reshape(n_phys, SB_W, PAGE)
    o_s = _sb_sample(q_s.reshape(per_seq), k_s.reshape(per_seq), v_s.reshape(per_seq),
                     cache_kt, cache_vt, page_table, sb_bias[0])
    ys = _sb_out(o_s.reshape(dec_b * dec_s, SB_W), g_s[0], xs_flat[0], w_out_b[0], tm=TQ)

    heads = (SB_HEADS, SB_DH)

    def from_feature_major(t):
        return jnp.transpose(t.reshape(bz, *heads, seq_p), (0, 3, 1, 2))[None]

    return (yp, ys.reshape(dec_b, dec_s, D_MODEL), st_p[None], st_s[None],
            from_feature_major(k_p), from_feature_major(v_p),
            k_s.reshape(1, dec_b, dec_s, *heads), v_s.reshape(1, dec_b, dec_s, *heads))
```
